```python
import math
import jax, jax.numpy as jnp
from jax import lax
import numpy as np

D_MODEL = 1024
BATCH = 1
SEQ = 16384
DEPTH = 2

MIX_WIDTH = D_MODEL
HG_WIDTH = D_MODEL // 4
HG_HEAD_DIM = 64
HG_HEADS = HG_WIDTH // HG_HEAD_DIM
HG_CHUNK = 128
S5_WIDTH = D_MODEL // 4
S5_GROUP = 16
S5_GROUPS = S5_WIDTH // S5_GROUP
S5_STATE = 64
S5_MIN_NEG = 1e-4
DA_WIDTH = D_MODEL // 2
DA_HEAD_DIM = 64
DA_HEADS = DA_WIDTH // (2 * DA_HEAD_DIM)
DA_QK_WIDTH = DA_HEADS * 2 * DA_HEAD_DIM
DA_BLOCK = 128
ROPE_THETA = 10000.0
D_FF = 4 * D_MODEL
EPS = 1e-6
IN_SIZES = (HG_WIDTH, HG_WIDTH, HG_WIDTH, HG_WIDTH, S5_WIDTH, DA_QK_WIDTH, DA_QK_WIDTH, DA_WIDTH)
IN_COLS = 4 * HG_WIDTH + S5_WIDTH + 2 * DA_QK_WIDTH + DA_WIDTH

kernel_name = "hymba_style_hgrn2_s5_diffattn_hybrid"


def rms_norm(x, gain):
    xf = x.astype(jnp.float32)
    var = jnp.mean(xf * xf, axis=-1, keepdims=True)
    return (xf * lax.rsqrt(var + EPS) * gain.astype(jnp.float32)).astype(x.dtype)


def hgrn2_mixer(q, f, i, g, lb, g_norm_gain):
    B, L, _ = q.shape
    nc = L // HG_CHUNK

    def heads(t):
        return t.astype(jnp.float32).reshape(B, nc, HG_CHUNK, HG_HEADS, HG_HEAD_DIM).transpose(1, 0, 3, 2, 4)

    lbf = jnp.clip(lb.astype(jnp.float32), 0.0, 1.0).reshape(1, 1, HG_HEADS, 1, HG_HEAD_DIM)
    qh = jax.nn.silu(heads(q))
    fr = heads(f)
    log_f = jnp.logaddexp(jnp.log(lbf), jnp.log1p(-lbf) + jax.nn.log_sigmoid(fr))
    kh = (1.0 - lbf) * jax.nn.sigmoid(-fr)
    vh = heads(i)
    causal = jnp.tril(jnp.ones((HG_CHUNK, HG_CHUNK), dtype=bool))[:, :, None]

    def step(S, inp):
        qc, kc, vc, lfc = inp
        b = jnp.cumsum(lfc, axis=2)
        o_inter = jnp.einsum('bhtk,bhkv->bhtv', qc * jnp.exp(b), S)
        rel = b[:, :, :, None, :] - b[:, :, None, :, :]
        decay = jnp.exp(jnp.where(causal, rel, -jnp.inf))
        scores = jnp.einsum('bhtk,bhsk,bhtsk->bhts', qc, kc, decay)
        o_intra = jnp.einsum('bhts,bhsv->bhtv', scores, vc)
        b_last = b[:, :, -1:, :]
        S_new = jnp.exp(b_last[:, :, 0, :])[..., None] * S + jnp.einsum(
            'bhsk,bhsv->bhkv', kc * jnp.exp(b_last - b), vc)
        return S_new, o_inter + o_intra

    S0 = jnp.zeros((B, HG_HEADS, HG_HEAD_DIM, HG_HEAD_DIM), jnp.float32)
    _, o = lax.scan(step, S0, (qh, kh, vh, log_f))
    o = o.transpose(1, 0, 3, 2, 4).reshape(B, L, HG_HEADS, HG_HEAD_DIM)
    gate = jax.nn.silu(g.astype(jnp.float32).reshape(B, L, HG_HEADS, HG_HEAD_DIM))
    o = rms_norm(o, g_norm_gain) * gate
    return o.reshape(B, L, HG_WIDTH)


def s5_mixer(u, a_re, a_im, log_dt, b_re, b_im, c_re, c_im, d_skip, w_glu, out_gain):
    B, L, _ = u.shape
    f32 = jnp.float32
    uf = u.astype(f32).reshape(B, L, S5_GROUPS, S5_GROUP)
    lam = lax.complex(jnp.minimum(a_re.astype(f32), -S5_MIN_NEG), a_im.astype(f32))
    dt = jnp.exp(log_dt.astype(f32))
    a_bar = jnp.exp(lam * dt)
    b_scale = (a_bar - 1.0) / lam
    b_bar = lax.complex(b_re.astype(f32), b_im.astype(f32)) * b_scale[..., None]
    bu = jnp.einsum('blgh,gph->blgp', uf.astype(jnp.complex64), b_bar)
    a_seq = jnp.broadcast_to(a_bar, bu.shape)

    def combine(left, right):
        a_l, b_l = left
        a_r, b_r = right
        return a_r * a_l, a_r * b_l + b_r

    _, states = lax.associative_scan(combine, (a_seq, bu), axis=1)
    c_mat = lax.complex(c_re.astype(f32), c_im.astype(f32))
    y = jnp.einsum('blgp,ghp->blgh', states, c_mat).real + d_skip.astype(f32) * uf
    y = jax.nn.gelu(y.reshape(B, L, S5_WIDTH))
    y = y * jax.nn.sigmoid(y @ w_glu.astype(f32))
    return rms_norm(y, out_gain)


def rope(t, cos, sin):
    t1, t2 = jnp.split(t, 2, axis=-1)
    return jnp.concatenate([t1 * cos - t2 * sin, t2 * cos + t1 * sin], axis=-1)


def diff_attention(q, k, v, positions, q_gain, k_gain, lq1, lk1, lq2, lk2, subln_gain, lambda_init):
    B, L, _ = q.shape
    f32 = jnp.float32
    q = q.astype(f32).reshape(B, L, DA_HEADS, 2, DA_HEAD_DIM)
    k = k.astype(f32).reshape(B, L, DA_HEADS, 2, DA_HEAD_DIM)
    v = v.astype(f32).reshape(B, L, DA_HEADS, 2 * DA_HEAD_DIM)
    q = rms_norm(q, q_gain)
    k = rms_norm(k, k_gain)
    inv_freq = ROPE_THETA ** (-jnp.arange(0, DA_HEAD_DIM, 2, dtype=f32) / DA_HEAD_DIM)
    ang = positions.astype(f32)[..., None] * inv_freq
    cos = jnp.cos(ang)[:, :, None, None, :]
    sin = jnp.sin(ang)[:, :, None, None, :]
    q = rope(q, cos, sin) * (DA_HEAD_DIM ** -0.5)
    k = rope(k, cos, sin)
    lam = (jnp.exp(jnp.sum(lq1.astype(f32) * lk1.astype(f32)))
           - jnp.exp(jnp.sum(lq2.astype(f32) * lk2.astype(f32))) + lambda_init)
    kt = k.transpose(0, 2, 3, 1, 4)
    vt = v.transpose(0, 2, 1, 3)
    nb = L // DA_BLOCK
    qb = q.reshape(B, nb, DA_BLOCK, DA_HEADS, 2, DA_HEAD_DIM).transpose(1, 0, 3, 4, 2, 5)
    pb = positions.reshape(B, nb, DA_BLOCK).transpose(1, 0, 2)

    def block(args):
        qblk, pos_q = args
        s = jnp.einsum('bhcqd,bhckd->bhcqk', qblk, kt)
        mask = positions[:, None, None, None, :] <= pos_q[:, None, None, :, None]
        p = jax.nn.softmax(jnp.where(mask, s, -jnp.inf), axis=-1)
        w = p[:, :, 0] - lam * p[:, :, 1]
        return jnp.einsum('bhqk,bhkv->bhqv', w, vt)

    o = lax.map(block, (qb, pb))
    o = o.transpose(1, 0, 3, 2, 4).reshape(B, L, DA_HEADS, 2 * DA_HEAD_DIM)
    o = rms_norm(o, subln_gain) * (1.0 - lambda_init)
    return o.reshape(B, L, DA_WIDTH)


def setup_inputs(seed: int = 0) -> dict:
    key = jax.random.key(seed)
    ks = jax.random.split(key, 32)
    f32 = jnp.float32
    nrm = lambda k, shape, s: jax.random.normal(k, shape, f32) * s
    gain = lambda k, shape: 1.0 + 0.02 * jax.random.normal(k, shape, f32)
    n_idx = jnp.arange(S5_STATE, dtype=f32)
    return {
        "x": jax.random.normal(ks[0], (BATCH, SEQ, D_MODEL), f32),
        "positions": jnp.broadcast_to(jnp.arange(SEQ, dtype=jnp.int32), (BATCH, SEQ)),
        "hgrn_lower_bounds": nrm(ks[1], (DEPTH, HG_WIDTH), 0.1),
        "attn_norm_gain": gain(ks[2], (DEPTH, D_MODEL)),
        "w_in": nrm(ks[3], (DEPTH, D_MODEL, IN_COLS), D_MODEL ** -0.5),
        "hg_norm_gain": gain(ks[4], (DEPTH, HG_HEAD_DIM)),
        "s5_a_re": -0.5 + nrm(ks[5], (DEPTH, S5_GROUPS, S5_STATE), 0.01),
        "s5_a_im": math.pi * n_idx + nrm(ks[6], (DEPTH, S5_GROUPS, S5_STATE), 0.01),
        "s5_log_dt": jax.random.uniform(ks[7], (DEPTH, S5_GROUPS, S5_STATE), f32,
                                        math.log(1e-3), math.log(1e-1)),
        "s5_b_re": nrm(ks[8], (DEPTH, S5_GROUPS, S5_STATE, S5_GROUP), (2 * S5_GROUP) ** -0.5),
        "s5_b_im": nrm(ks[9], (DEPTH, S5_GROUPS, S5_STATE, S5_GROUP), (2 * S5_GROUP) ** -0.5),
        "s5_c_re": nrm(ks[10], (DEPTH, S5_GROUPS, S5_GROUP, S5_STATE), S5_STATE ** -0.5),
        "s5_c_im": nrm(ks[11], (DEPTH, S5_GROUPS, S5_GROUP, S5_STATE), S5_STATE ** -0.5),
        "s5_d": nrm(ks[12], (DEPTH, S5_GROUPS, S5_GROUP), 1.0),
        "s5_w_glu": nrm(ks[13], (DEPTH, S5_WIDTH, S5_WIDTH), S5_WIDTH ** -0.5),
        "s5_norm_gain": gain(ks[14], (DEPTH, S5_WIDTH)),
        "da_q_norm_gain": gain(ks[15], (DEPTH, DA_HEAD_DIM)),
        "da_k_norm_gain": gain(ks[16], (DEPTH, DA_HEAD_DIM)),
        "da_lambda_q1": nrm(ks[17], (DEPTH, DA_HEAD_DIM), 0.1),
        "da_lambda_k1": nrm(ks[18], (DEPTH, DA_HEAD_DIM), 0.1),
        "da_lambda_q2": nrm(ks[19], (DEPTH, DA_HEAD_DIM), 0.1),
        "da_lambda_k2": nrm(ks[20], (DEPTH, DA_HEAD_DIM), 0.1),
        "da_subln_gain": gain(ks[21], (DEPTH, 2 * DA_HEAD_DIM)),
        "w_out": nrm(ks[22], (DEPTH, MIX_WIDTH, D_MODEL), MIX_WIDTH ** -0.5),
        "mlp_norm_gain": gain(ks[23], (DEPTH, D_MODEL)),
        "w_mlp_up": nrm(ks[24], (DEPTH, D_MODEL, D_FF), D_MODEL ** -0.5),
        "w_mlp_down": nrm(ks[25], (DEPTH, D_FF, D_MODEL), D_FF ** -0.5),
    }


def reference(x, positions, hgrn_lower_bounds, attn_norm_gain, w_in, hg_norm_gain,
              s5_a_re, s5_a_im, s5_log_dt, s5_b_re, s5_b_im, s5_c_re, s5_c_im, s5_d,
              s5_w_glu, s5_norm_gain, da_q_norm_gain, da_k_norm_gain, da_lambda_q1,
              da_lambda_k1, da_lambda_q2, da_lambda_k2, da_subln_gain, w_out,
              mlp_norm_gain, w_mlp_up, w_mlp_down):
    lb_all = jnp.cumsum(jax.nn.softmax(hgrn_lower_bounds.astype(jnp.float32), axis=0), axis=0)
    lb_all = lb_all - lb_all[0]
    split_points = [int(v) for v in np.cumsum(IN_SIZES)[:-1]]
    for l in range(DEPTH):
        lambda_init = 0.8 - 0.6 * math.exp(-0.3 * l)
        h = rms_norm(x, attn_norm_gain[l])
        proj = h @ w_in[l]
        hq, hf, hi, hg, su, dq, dk, dv = jnp.split(proj, split_points, axis=-1)
        o_a = hgrn2_mixer(hq, hf, hi, hg, lb_all[l], hg_norm_gain[l])
        o_b = s5_mixer(su, s5_a_re[l], s5_a_im[l], s5_log_dt[l], s5_b_re[l], s5_b_im[l],
                       s5_c_re[l], s5_c_im[l], s5_d[l], s5_w_glu[l], s5_norm_gain[l])
        o_c = diff_attention(dq, dk, dv, positions, da_q_norm_gain[l], da_k_norm_gain[l],
                             da_lambda_q1[l], da_lambda_k1[l], da_lambda_q2[l], da_lambda_k2[l],
                             da_subln_gain[l], lambda_init)
        mixed = jnp.concatenate([o_a, o_b, o_c], axis=-1).astype(x.dtype)
        x = x + mixed @ w_out[l]
        h = rms_norm(x, mlp_norm_gain[l])
        x = x + jnp.square(jax.nn.relu(h @ w_mlp_up[l])) @ w_mlp_down[l]
    return x
```

```python
import functools
import math

import numpy as np
import jax
import jax.numpy as jnp
from jax import lax
from jax.experimental import pallas as pl
from jax.experimental.pallas import tpu as pltpu

F32 = jnp.float32
BF16 = jnp.bfloat16

EPS = 1e-6
ROPE_THETA = 10000.0
S5_MIN_NEG = 1e-4

HEAD_DIM = 64
HG_CHUNK = 128
HG_LEVELS = 7
S5_GROUP = 16
S5_STATE = 64
SUBLANES = 8
LANES = 128

VMEM_LIMIT = 56 * 1024 * 1024


def _cparams(semantics):
    return pltpu.CompilerParams(dimension_semantics=semantics, vmem_limit_bytes=VMEM_LIMIT)


def _dot(a, b):
    return jnp.dot(a, b, preferred_element_type=F32)


def _dot_nt(a, b):
    return lax.dot_general(a, b, (((1,), (1,)), ((), ())), preferred_element_type=F32)


def _dot_tn(a, b):
    return lax.dot_general(a, b, (((0,), (0,)), ((), ())), preferred_element_type=F32)


def _split_bf16(t):
    hi = t.astype(BF16)
    lo = (t - hi.astype(F32)).astype(BF16)
    return hi, lo


def _group_sum(t, ones_blockdiag):
    hi, lo = _split_bf16(t)
    return _dot(hi, ones_blockdiag) + _dot(lo, ones_blockdiag)


def _rope_kernel(pos_ref, invf_ref, cos_ref, sin_ref):
    ang = pos_ref[...].astype(F32) * invf_ref[...]
    lane = lax.broadcasted_iota(jnp.int32, ang.shape, 1)
    first_half = (lane % HEAD_DIM) < (HEAD_DIM // 2)
    s = jnp.sin(ang)
    cos_ref[...] = jnp.cos(ang)
    sin_ref[...] = jnp.where(first_half, -s, s)


def _rope_tables(pos_col):
    L = pos_col.shape[0]
    tr = min(L, 2048)
    inv_freq = ROPE_THETA ** (-jnp.arange(0, HEAD_DIM, 2, dtype=F32) / HEAD_DIM)
    invf = jnp.tile(inv_freq, LANES // (HEAD_DIM // 2))[None, :]
    return pl.pallas_call(
        _rope_kernel,
        out_shape=(jax.ShapeDtypeStruct((L, LANES), F32), jax.ShapeDtypeStruct((L, LANES), F32)),
        grid=(L // tr,),
        in_specs=[pl.BlockSpec((tr, 1), lambda i: (i, 0)), pl.BlockSpec((1, LANES), lambda i: (0, 0))],
        out_specs=(pl.BlockSpec((tr, LANES), lambda i: (i, 0)), pl.BlockSpec((tr, LANES), lambda i: (i, 0))),
        compiler_params=_cparams(("parallel",)),
        name="rope_tables",
    )(pos_col, invf)


def _proj_kernel(x_ref, g_ref, w_ref, cos_ref, sin_ref, qg_ref, kg_ref, ones_ref,
                 hg_ref, su_ref, q_ref, k_ref, v_ref, *, splits):
    x = x_ref[...]
    var = jnp.mean(x * x, axis=-1, keepdims=True)
    h = (x * lax.rsqrt(var + EPS) * g_ref[...]).astype(BF16)
    c0, c1, c2, c3, c4, c5 = splits
    hg_ref[...] = _dot(h, w_ref[:, c0:c1])
    su_ref[...] = _dot(h, w_ref[:, c1:c2])
    v_ref[...] = _dot(h, w_ref[:, c4:c5]).astype(BF16)

    cos = cos_ref[...]
    sin = sin_ref[...]
    lane = lax.broadcasted_iota(jnp.int32, cos.shape, 1)
    first_half = (lane % HEAD_DIM) < (HEAD_DIM // 2)
    ones = ones_ref[...]

    def norm_rope(t, gain, scale):
        ms = _group_sum(t * t, ones) * (1.0 / HEAD_DIM)
        tn = t * lax.rsqrt(ms + EPS) * gain
        outs = []
        for j in range(t.shape[1] // LANES):
            tj = tn[:, j * LANES:(j + 1) * LANES]
            partner = jnp.where(first_half,
                                pltpu.roll(tj, LANES - HEAD_DIM // 2, 1),
                                pltpu.roll(tj, HEAD_DIM // 2, 1))
            outs.append((tj * cos + partner * sin) * scale)
        return jnp.concatenate(outs, axis=1)

    q_ref[...] = norm_rope(_dot(h, w_ref[:, c2:c3]), qg_ref[...], HEAD_DIM ** -0.5).astype(BF16)
    k_ref[...] = norm_rope(_dot(h, w_ref[:, c3:c4]), kg_ref[...], 1.0).astype(BF16)


def _proj(x2, gain, w_bf16, cos, sin, q_gain, k_gain, *, tm):
    L, D = x2.shape
    ncols = w_bf16.shape[1]
    hgw, s5w, qkw, vw = 4 * (D // 4), D // 4, D // 2, D // 2
    splits = (0, hgw, hgw + s5w, hgw + s5w + qkw, hgw + s5w + 2 * qkw, hgw + s5w + 2 * qkw + vw)
    assert splits[-1] == ncols
    ones = jnp.asarray(np.kron(np.eye(qkw // HEAD_DIM), np.ones((HEAD_DIM, HEAD_DIM))).astype(np.float32)).astype(BF16)
    qg = jnp.tile(q_gain.astype(F32), qkw // HEAD_DIM)[None, :]
    kg = jnp.tile(k_gain.astype(F32), qkw // HEAD_DIM)[None, :]
    row = lambda w: pl.BlockSpec((tm, w), lambda i: (i, 0))
    full = lambda a: pl.BlockSpec(a.shape, lambda i: (0,) * a.ndim)
    gain2 = gain.astype(F32)[None, :]
    return pl.pallas_call(
        functools.partial(_proj_kernel, splits=splits),
        out_shape=(jax.ShapeDtypeStruct((L, hgw), F32), jax.ShapeDtypeStruct((L, s5w), F32),
                   jax.ShapeDtypeStruct((L, qkw), BF16), jax.ShapeDtypeStruct((L, qkw), BF16),
                   jax.ShapeDtypeStruct((L, vw), BF16)),
        grid=(L // tm,),
        in_specs=[row(D), full(gain2), full(w_bf16), row(LANES), row(LANES), full(qg), full(kg), full(ones)],
        out_specs=(row(hgw), row(s5w), row(qkw), row(qkw), row(vw)),
        compiler_params=_cparams(("parallel",)),
        name="in_proj",
    )(x2, gain2, w_bf16, cos, sin, qg, kg, ones)


def _hgrn_kernel(hg_ref, lbraw_ref, gn_ref, mstack_ref, pmask_ref, ones_ref, o_ref, st_ref,
                 *, layer, nchunk, width, heads):
    @pl.when(pl.program_id(0) == 0)
    def _():
        st_ref[...] = jnp.zeros_like(st_ref)

    lbr = lbraw_ref[...]
    e = jnp.exp(lbr - jnp.max(lbr, axis=0, keepdims=True))
    sm = e / jnp.sum(e, axis=0, keepdims=True)
    lb = jnp.zeros((1, width), F32)
    for r in range(1, layer + 1):
        lb = lb + sm[r:r + 1, :]
    lbf = jnp.clip(lb, 0.0, 1.0)
    log_lb = jnp.log(lbf)
    log_1m_lb = jnp.log1p(-lbf)

    row = lax.broadcasted_iota(jnp.int32, (HG_CHUNK, 1), 0)
    gn = gn_ref[...]
    ones = ones_ref[...]

    def chunk(c, carry):
        rows = pl.ds(pl.multiple_of(c * HG_CHUNK, HG_CHUNK), HG_CHUNK)
        q = hg_ref[rows, 0:width]
        fr = hg_ref[rows, width:2 * width]
        vi = hg_ref[rows, 2 * width:3 * width]
        g = hg_ref[rows, 3 * width:4 * width]
        qh = q * jax.nn.sigmoid(q)
        log_sig = jnp.minimum(fr, 0.0) - jnp.log1p(jnp.exp(-jnp.abs(fr)))
        t2 = log_1m_lb + log_sig
        lf = jnp.maximum(log_lb, t2) + jnp.log1p(jnp.exp(-jnp.abs(log_lb - t2)))
        kh = (1.0 - lbf) * jax.nn.sigmoid(-fr)

        hi, lo = _split_bf16(lf)
        r = _dot(mstack_ref[...], jnp.concatenate([hi, lo], axis=1))
        r = r[:, 0:width] + r[:, width:2 * width]
        b = r[0:HG_CHUNK]
        b_last = b[HG_CHUNK - 1:HG_CHUNK, :]

        qls, kls = [], []
        for lv in range(HG_LEVELS):
            hblk = 1 << lv
            w = jnp.exp(-jnp.abs(b - r[(lv + 1) * HG_CHUNK:(lv + 2) * HG_CHUNK]))
            odd = ((row // hblk) % 2) == 1
            qls.append(jnp.where(odd, qh * w, 0.0).astype(BF16))
            kls.append(jnp.where(odd, 0.0, kh * w).astype(BF16))

        o_diag = _group_sum(qh * kh, ones) * vi
        qe = (qh * jnp.exp(b)).astype(BF16)
        kd = (kh * jnp.exp(b_last - b)).astype(BF16)
        dec = jnp.exp(b_last)
        vb = vi.astype(BF16)

        outs = []
        for hd in range(heads):
            hs = slice(hd * HEAD_DIM, (hd + 1) * HEAD_DIM)
            sc = jnp.zeros((HG_CHUNK, HG_CHUNK), F32)
            for lv in range(HG_LEVELS):
                sc = sc + pmask_ref[lv] * _dot_nt(qls[lv][:, hs], kls[lv][:, hs])
            st = st_ref[hd]
            v_h = vb[:, hs]
            o_h = _dot(sc.astype(BF16), v_h) + _dot_nt(qe[:, hs], st.astype(BF16)) + o_diag[:, hs]
            st_ref[hd] = st * dec[:, hs] + _dot_tn(v_h, kd[:, hs])
            ms = jnp.mean(o_h * o_h, axis=-1, keepdims=True)
            outs.append(o_h * lax.rsqrt(ms + EPS) * gn)
        o = jnp.concatenate(outs, axis=1) * (g * jax.nn.sigmoid(g))
        o_ref[rows, :] = o.astype(BF16)
        return carry

    lax.fori_loop(0, nchunk, chunk, 0)


def _hgrn_constants():
    t = np.arange(HG_CHUNK)
    mats = [(t[None, :] <= t[:, None]).astype(np.float32)]
    pmasks = []
    for lv in range(HG_LEVELS):
        hblk = 1 << lv
        boundary = (t // (2 * hblk)) * 2 * hblk + hblk - 1
        mats.append((t[None, :] <= boundary[:, None]).astype(np.float32))
        pmasks.append((t[None, :] // (2 * hblk) == t[:, None] // (2 * hblk)).astype(np.float32))
    return np.concatenate(mats, axis=0), np.stack(pmasks, axis=0)


def _hgrn(hg, lb_raw, gn_gain, *, layer, tt):
    L, w4 = hg.shape
    width = w4 // 4
    heads = width // HEAD_DIM
    mstack_np, pmask_np = _hgrn_constants()
    mstack = jnp.asarray(mstack_np).astype(BF16)
    pmask = jnp.asarray(pmask_np)
    ones = jnp.asarray(np.kron(np.eye(heads), np.ones((HEAD_DIM, HEAD_DIM))).astype(np.float32)).astype(BF16)
    lbr = lb_raw.astype(F32)
    gn = gn_gain.astype(F32)[None, :]
    full = lambda a: pl.BlockSpec(a.shape, lambda i: (0,) * a.ndim)
    return pl.pallas_call(
        functools.partial(_hgrn_kernel, layer=layer, nchunk=tt // HG_CHUNK, width=width, heads=heads),
        out_shape=jax.ShapeDtypeStruct((L, width), BF16),
        grid=(L // tt,),
        in_specs=[pl.BlockSpec((tt, w4), lambda i: (i, 0)), full(lbr), full(gn), full(mstack), full(pmask), full(ones)],
        out_specs=pl.BlockSpec((tt, width), lambda i: (i, 0)),
        scratch_shapes=[pltpu.VMEM((heads, HEAD_DIM, HEAD_DIM), F32)],
        compiler_params=_cparams(("arbitrary",)),
        name="hgrn2",
    )(hg, lbr, gn, mstack, pmask, ones)


def _s5_kernel(u_ref, bblk_ref, cblk_ref, d_ref, tr_ref, ti_ref, pr_ref, pi_ref, wglu_ref, gain_ref,
               o_ref, bu_ref, cr_ref, ci_ref, *, nblk, nstate):
    @pl.when(pl.program_id(0) == 0)
    def _():
        cr_ref[...] = jnp.zeros_like(cr_ref)
        ci_ref[...] = jnp.zeros_like(ci_ref)

    u = u_ref[...]
    bu_ref[...] = _dot(u.astype(BF16), bblk_ref[...])
    pr = pr_ref[...]
    pi = pi_ref[...]

    def blk(i, carry):
        cr, ci = carry
        rows = pl.ds(pl.multiple_of(i * SUBLANES, SUBLANES), SUBLANES)
        xr = bu_ref[rows, 0:nstate]
        xi = bu_ref[rows, nstate:2 * nstate]
        for lv in range(3):
            sh = 1 << lv
            sr = pltpu.roll(xr, sh, 0)
            si = pltpu.roll(xi, sh, 0)
            tr = tr_ref[lv]
            ti = ti_ref[lv]
            xr, xi = xr + (tr * sr - ti * si), xi + (tr * si + ti * sr)
        xr, xi = xr + (pr * cr - pi * ci), xi + (pr * ci + pi * cr)
        bu_ref[rows, 0:nstate] = xr
        bu_ref[rows, nstate:2 * nstate] = xi
        return xr[SUBLANES - 1:SUBLANES, :], xi[SUBLANES - 1:SUBLANES, :]

    cr, ci = lax.fori_loop(0, nblk, blk, (cr_ref[...], ci_ref[...]))
    cr_ref[...] = cr
    ci_ref[...] = ci

    y = _dot(bu_ref[...].astype(BF16), cblk_ref[...]) + d_ref[...] * u
    y = jax.nn.gelu(y)
    z = y * jax.nn.sigmoid(_dot(y.astype(BF16), wglu_ref[...]))
    var = jnp.mean(z * z, axis=-1, keepdims=True)
    o_ref[...] = (z * lax.rsqrt(var + EPS) * gain_ref[...]).astype(BF16)


def _s5(u, a_re, a_im, log_dt, b_re, b_im, c_re, c_im, d_skip, w_glu, out_gain, *, tt):
    L, width = u.shape
    G, P = a_re.shape
    nstate = G * P
    lam_re = jnp.minimum(a_re.astype(F32), -S5_MIN_NEG)
    lam_im = a_im.astype(F32)
    dt = jnp.exp(log_dt.astype(F32))
    mag = jnp.exp(lam_re * dt)
    abar_re, abar_im = mag * jnp.cos(lam_im * dt), mag * jnp.sin(lam_im * dt)
    den = lam_re * lam_re + lam_im * lam_im
    sc_re = ((abar_re - 1.0) * lam_re + abar_im * lam_im) / den
    sc_im = (abar_im * lam_re - (abar_re - 1.0) * lam_im) / den
    bre, bim = b_re.astype(F32), b_im.astype(F32)
    bbar_re = bre * sc_re[..., None] - bim * sc_im[..., None]
    bbar_im = bre * sc_im[..., None] + bim * sc_re[..., None]
    eye = jnp.eye(G, dtype=F32)
    bd = lambda m: jnp.einsum('gph,gk->ghkp', m, eye).reshape(G * S5_GROUP, nstate)
    bblk = jnp.concatenate([bd(bbar_re), bd(bbar_im)], axis=1).astype(BF16)
    cd = lambda m: jnp.einsum('ghp,gk->gpkh', m, eye).reshape(nstate, G * S5_GROUP)
    cblk = jnp.concatenate([cd(c_re.astype(F32)), cd(-c_im.astype(F32))], axis=0).astype(BF16)
    kpow = jnp.arange(1, SUBLANES + 1, dtype=F32)[:, None]
    pmag = jnp.exp((lam_re * dt).reshape(1, nstate) * kpow)
    parg = (lam_im * dt).reshape(1, nstate) * kpow
    p_re, p_im = pmag * jnp.cos(parg), pmag * jnp.sin(parg)
    rowi = np.arange(SUBLANES)[:, None]
    tmask = jnp.asarray(np.stack([(rowi >= (1 << lv)).astype(np.float32) for lv in range(3)]))
    t_re = tmask * jnp.stack([p_re[(1 << lv) - 1] for lv in range(3)])[:, None, :]
    t_im = tmask * jnp.stack([p_im[(1 << lv) - 1] for lv in range(3)])[:, None, :]
    d2 = d_skip.astype(F32).reshape(1, width)
    gain2 = out_gain.astype(F32)[None, :]
    wglu = w_glu.astype(BF16)
    full = lambda a: pl.BlockSpec(a.shape, lambda i: (0,) * a.ndim)
    return pl.pallas_call(
        functools.partial(_s5_kernel, nblk=tt // SUBLANES, nstate=nstate),
        out_shape=jax.ShapeDtypeStruct((L, width), BF16),
        grid=(L // tt,),
        in_specs=[pl.BlockSpec((tt, width), lambda i: (i, 0)), full(bblk), full(cblk), full(d2),
                  full(t_re), full(t_im), full(p_re), full(p_im), full(wglu), full(gain2)],
        out_specs=pl.BlockSpec((tt, width), lambda i: (i, 0)),
        scratch_shapes=[pltpu.VMEM((tt, 2 * nstate), F32), pltpu.VMEM((1, nstate), F32), pltpu.VMEM((1, nstate), F32)],
        compiler_params=_cparams(("arbitrary",)),
        name="s5",
    )(u, bblk, cblk, d2, t_re, t_im, p_re, p_im, wglu, gain2)


def _attn_kernel(posq_ref, posk_ref, q_ref, k_ref, v_ref, lq1_ref, lk1_ref, lq2_ref, lk2_ref, sg_ref,
                 o_ref, m_ref, l_ref, acc_ref, *, tq, lambda_init):
    qi = pl.program_id(1)
    q = q_ref[...]
    lane = lax.broadcasted_iota(jnp.int32, q.shape, 1)
    zero = jnp.zeros_like(q)
    q2 = jnp.concatenate([jnp.where(lane < HEAD_DIM, q, zero), jnp.where(lane < HEAD_DIM, zero, q)], axis=0)

    kv0 = pl.multiple_of(qi * tq, tq)
    s = _dot_nt(q2, k_ref[pl.ds(kv0, tq), :])
    mask = posk_ref[...] <= posq_ref[...]
    s = jnp.where(jnp.concatenate([mask, mask], axis=0), s, -jnp.inf)
    m = jnp.max(s, axis=-1, keepdims=True)
    p = jnp.exp(s - m)
    m_ref[...] = m
    l_ref[...] = jnp.sum(p, axis=-1, keepdims=True)
    acc_ref[...] = _dot(p.astype(BF16), v_ref[pl.ds(kv0, tq), :])

    def body(j, carry):
        kv = pl.multiple_of(j * tq, tq)
        s = _dot_nt(q2, k_ref[pl.ds(kv, tq), :])
        m_old = m_ref[...]
        m_new = jnp.maximum(m_old, jnp.max(s, axis=-1, keepdims=True))
        alpha = jnp.exp(m_old - m_new)
        p = jnp.exp(s - m_new)
        m_ref[...] = m_new
        l_ref[...] = alpha * l_ref[...] + jnp.sum(p, axis=-1, keepdims=True)
        acc_ref[...] = alpha * acc_ref[...] + _dot(p.astype(BF16), v_ref[pl.ds(kv, tq), :])
        return carry

    lax.fori_loop(0, qi, body, 0)

    lam = (jnp.exp(jnp.sum(lq1_ref[...] * lk1_ref[...], axis=-1, keepdims=True))
           - jnp.exp(jnp.sum(lq2_ref[...] * lk2_ref[...], axis=-1, keepdims=True)) + lambda_init)
    o = acc_ref[...] / l_ref[...]
    o = o[0:tq] - lam * o[tq:2 * tq]
    ms = jnp.mean(o * o, axis=-1, keepdims=True)
    o_ref[...] = (o * lax.rsqrt(ms + EPS) * sg_ref[...] * (1.0 - lambda_init)).astype(BF16)


def _attn(q, k, v, pos_col, pos_row, lq1, lk1, lq2, lk2, subln_gain, *, lambda_init, tq):
    L, qkw = q.shape
    heads = qkw // (2 * HEAD_DIM)
    vec = lambda a: a.astype(F32)[None, :]
    lq1, lk1, lq2, lk2, sg = vec(lq1), vec(lk1), vec(lq2), vec(lk2), vec(subln_gain)
    full = lambda a: pl.BlockSpec(a.shape, lambda h, i: (0,) * a.ndim)
    return pl.pallas_call(
        functools.partial(_attn_kernel, tq=tq, lambda_init=lambda_init),
        out_shape=jax.ShapeDtypeStruct((L, qkw), BF16),
        grid=(heads, L // tq),
        in_specs=[pl.BlockSpec((tq, 1), lambda h, i: (i, 0)),
                  pl.BlockSpec((1, tq), lambda h, i: (0, i)),
                  pl.BlockSpec((tq, LANES), lambda h, i: (i, h)),
                  pl.BlockSpec((L, LANES), lambda h, i: (0, h)),
                  pl.BlockSpec((L, LANES), lambda h, i: (0, h)),
                  full(lq1), full(lk1), full(lq2), full(lk2), full(sg)],
        out_specs=pl.BlockSpec((tq, LANES), lambda h, i: (i, h)),
        scratch_shapes=[pltpu.VMEM((2 * tq, 1), F32), pltpu.VMEM((2 * tq, 1), F32), pltpu.VMEM((2 * tq, LANES), F32)],
        compiler_params=_cparams(("arbitrary", "arbitrary")),
        name="diff_attn",
    )(pos_col, pos_row, q, k, v, lq1, lk1, lq2, lk2, sg)


def _mlp_kernel(x_ref, oa_ref, ob_ref, oc_ref, wout_ref, g_ref, wup_ref, wdn_ref, o_ref,
                xres_ref, h_ref, acc_ref, *, wa, wb):
    k = pl.program_id(1)

    @pl.when(k == 0)
    def _():
        xn = (x_ref[...] + _dot(oa_ref[...], wout_ref[0:wa, :]) + _dot(ob_ref[...], wout_ref[wa:wa + wb, :])
              + _dot(oc_ref[...], wout_ref[wa + wb:, :]))
        xres_ref[...] = xn
        var = jnp.mean(xn * xn, axis=-1, keepdims=True)
        h_ref[...] = (xn * lax.rsqrt(var + EPS) * g_ref[...]).astype(BF16)
        acc_ref[...] = jnp.zeros_like(acc_ref)

    up = _dot(h_ref[...], wup_ref[...])
    act = jnp.square(jnp.maximum(up, 0.0))
    acc_ref[...] += _dot(act.astype(BF16), wdn_ref[...])

    @pl.when(k == pl.num_programs(1) - 1)
    def _():
        o_ref[...] = xres_ref[...] + acc_ref[...]


def _mlp(x2, oa, ob, oc, wout, gain, wup, wdn, *, tm, tf):
    L, D = x2.shape
    dff = wup.shape[1]
    wa, wb, wc = oa.shape[1], ob.shape[1], oc.shape[1]
    gain2 = gain.astype(F32)[None, :]
    row = lambda w: pl.BlockSpec((tm, w), lambda i, k: (i, 0))
    return pl.pallas_call(
        functools.partial(_mlp_kernel, wa=wa, wb=wb),
        out_shape=jax.ShapeDtypeStruct((L, D), F32),
        grid=(L // tm, dff // tf),
        in_specs=[row(D), row(wa), row(wb), row(wc),
                  pl.BlockSpec((D, D), lambda i, k: (0, 0)), pl.BlockSpec((1, D), lambda i, k: (0, 0)),
                  pl.BlockSpec((D, tf), lambda i, k: (0, k)), pl.BlockSpec((tf, D), lambda i, k: (k, 0))],
        out_specs=row(D),
        scratch_shapes=[pltpu.VMEM((tm, D), F32), pltpu.VMEM((tm, D), BF16), pltpu.VMEM((tm, D), F32)],
        compiler_params=_cparams(("parallel", "arbitrary")),
        name="out_proj_mlp",
    )(x2, oa, ob, oc, wout, gain2, wup, wdn)


def kernel(x, positions, hgrn_lower_bounds, attn_norm_gain, w_in, hg_norm_gain, s5_a_re, s5_a_im, s5_log_dt, s5_b_re, s5_b_im, s5_c_re, s5_c_im, s5_d, s5_w_glu, s5_norm_gain, da_q_norm_gain, da_k_norm_gain, da_lambda_q1, da_lambda_k1, da_lambda_q2, da_lambda_k2, da_subln_gain, w_out, mlp_norm_gain, w_mlp_up, w_mlp_down):
    batch, L, D = x.shape
    assert batch == 1
    depth = w_in.shape[0]
    tile = min(L, 512)
    x2 = x.reshape(L, D)
    pos_col = positions.reshape(L, 1)
    pos_row = positions.reshape(1, L)
    cos, sin = _rope_tables(pos_col)
    for l in range(depth):
        lambda_init = 0.8 - 0.6 * math.exp(-0.3 * l)
        hg, su, q, k, v = _proj(x2, attn_norm_gain[l], w_in[l].astype(BF16), cos, sin,
                                da_q_norm_gain[l], da_k_norm_gain[l], tm=tile)
        o_a = _hgrn(hg, hgrn_lower_bounds, hg_norm_gain[l], layer=l, tt=tile)
        o_b = _s5(su, s5_a_re[l], s5_a_im[l], s5_log_dt[l], s5_b_re[l], s5_b_im[l], s5_c_re[l], s5_c_im[l],
                  s5_d[l], s5_w_glu[l], s5_norm_gain[l], tt=tile)
        o_c = _attn(q, k, v, pos_col, pos_row, da_lambda_q1[l], da_lambda_k1[l], da_lambda_q2[l],
                    da_lambda_k2[l], da_subln_gain[l], lambda_init=lambda_init, tq=tile)
        x2 = _mlp(x2, o_a, o_b, o_c, w_out[l].astype(BF16), mlp_norm_gain[l],
                  w_mlp_up[l].astype(BF16), w_mlp_down[l].astype(BF16), tm=tile, tf=1024)
    return x2.reshape(batch, L, D)
```

```python
import functools
import math

import numpy as np
import jax
import jax.numpy as jnp
from jax import lax
from jax.experimental import pallas as pl
from jax.experimental.pallas import tpu as pltpu

F32 = jnp.float32
BF16 = jnp.bfloat16

EPS = 1e-6
ROPE_THETA = 10000.0
S5_MIN_NEG = 1e-4

HEAD_DIM = 64
HG_CHUNK = 128
HG_LEVELS = 7
S5_GROUP = 16
S5_STATE = 64
SUBLANES = 8
LANES = 128

VMEM_LIMIT = 56 * 1024 * 1024


def _cparams(semantics):
    return pltpu.CompilerParams(dimension_semantics=semantics, vmem_limit_bytes=VMEM_LIMIT)


def _dot(a, b):
    return jnp.dot(a, b, preferred_element_type=F32)


def _dot_nt(a, b):
    return lax.dot_general(a, b, (((1,), (1,)), ((), ())), preferred_element_type=F32)


def _dot_tn(a, b):
    return lax.dot_general(a, b, (((0,), (0,)), ((), ())), preferred_element_type=F32)


def _split_bf16(t):
    hi = t.astype(BF16)
    lo = (t - hi.astype(F32)).astype(BF16)
    return hi, lo


def _group_sum(t, ones_blockdiag):
    hi, lo = _split_bf16(t)
    return _dot(hi, ones_blockdiag) + _dot(lo, ones_blockdiag)


def _rope_kernel(pos_ref, invf_ref, cos_ref, sin_ref):
    ang = pos_ref[...].astype(F32) * invf_ref[...]
    lane = lax.broadcasted_iota(jnp.int32, ang.shape, 1)
    first_half = (lane % HEAD_DIM) < (HEAD_DIM // 2)
    s = jnp.sin(ang)
    cos_ref[...] = jnp.cos(ang)
    sin_ref[...] = jnp.where(first_half, -s, s)


def _rope_tables(pos_col):
    L = pos_col.shape[0]
    tr = min(L, 2048)
    inv_freq = ROPE_THETA ** (-jnp.arange(0, HEAD_DIM, 2, dtype=F32) / HEAD_DIM)
    invf = jnp.tile(inv_freq, LANES // (HEAD_DIM // 2))[None, :]
    return pl.pallas_call(
        _rope_kernel,
        out_shape=(jax.ShapeDtypeStruct((L, LANES), F32), jax.ShapeDtypeStruct((L, LANES), F32)),
        grid=(L // tr,),
        in_specs=[pl.BlockSpec((tr, 1), lambda i: (i, 0)), pl.BlockSpec((1, LANES), lambda i: (0, 0))],
        out_specs=(pl.BlockSpec((tr, LANES), lambda i: (i, 0)), pl.BlockSpec((tr, LANES), lambda i: (i, 0))),
        compiler_params=_cparams(("parallel",)),
        name="rope_tables",
    )(pos_col, invf)


def _proj_kernel(x_ref, g_ref, w_ref, cos_ref, sin_ref, qg_ref, kg_ref, ones_ref,
                 hg_ref, su_ref, q_ref, k_ref, v_ref, *, splits):
    x = x_ref[...]
    var = jnp.mean(x * x, axis=-1, keepdims=True)
    h = (x * lax.rsqrt(var + EPS) * g_ref[...]).astype(BF16)
    c0, c1, c2, c3, c4, c5 = splits
    hg_ref[...] = _dot(h, w_ref[:, c0:c1])
    su_ref[...] = _dot(h, w_ref[:, c1:c2])
    v_ref[0] = jnp.transpose(_dot(h, w_ref[:, c4:c5])).astype(BF16)

    cos = cos_ref[...]
    sin = sin_ref[...]
    lane = lax.broadcasted_iota(jnp.int32, cos.shape, 1)
    first_half = (lane % HEAD_DIM) < (HEAD_DIM // 2)
    ones = ones_ref[...]

    def norm_rope(t, gain, scale):
        ms = _group_sum(t * t, ones) * (1.0 / HEAD_DIM)
        tn = t * lax.rsqrt(ms + EPS) * gain
        outs = []
        for j in range(t.shape[1] // LANES):
            tj = tn[:, j * LANES:(j + 1) * LANES]
            partner = jnp.where(first_half,
                                pltpu.roll(tj, LANES - HEAD_DIM // 2, 1),
                                pltpu.roll(tj, HEAD_DIM // 2, 1))
            outs.append((tj * cos + partner * sin) * scale)
        return jnp.concatenate(outs, axis=1)

    q_ref[...] = norm_rope(_dot(h, w_ref[:, c2:c3]), qg_ref[...], HEAD_DIM ** -0.5).astype(BF16)
    k_ref[...] = norm_rope(_dot(h, w_ref[:, c3:c4]), kg_ref[...], 1.0).astype(BF16)


def _proj(x2, gain, w_bf16, cos, sin, q_gain, k_gain, *, tm):
    L, D = x2.shape
    ncols = w_bf16.shape[1]
    hgw, s5w, qkw, vw = 4 * (D // 4), D // 4, D // 2, D // 2
    splits = (0, hgw, hgw + s5w, hgw + s5w + qkw, hgw + s5w + 2 * qkw, hgw + s5w + 2 * qkw + vw)
    assert splits[-1] == ncols
    ones = jnp.asarray(np.kron(np.eye(qkw // HEAD_DIM), np.ones((HEAD_DIM, HEAD_DIM))).astype(np.float32)).astype(BF16)
    qg = jnp.tile(q_gain.astype(F32), qkw // HEAD_DIM)[None, :]
    kg = jnp.tile(k_gain.astype(F32), qkw // HEAD_DIM)[None, :]
    row = lambda w: pl.BlockSpec((tm, w), lambda i: (i, 0))
    full = lambda a: pl.BlockSpec(a.shape, lambda i: (0,) * a.ndim)
    gain2 = gain.astype(F32)[None, :]
    return pl.pallas_call(
        functools.partial(_proj_kernel, splits=splits),
        out_shape=(jax.ShapeDtypeStruct((L, hgw), F32), jax.ShapeDtypeStruct((L, s5w), F32),
                   jax.ShapeDtypeStruct((L, qkw), BF16), jax.ShapeDtypeStruct((L, qkw), BF16),
                   jax.ShapeDtypeStruct((L // tm, vw, tm), BF16)),
        grid=(L // tm,),
        in_specs=[row(D), full(gain2), full(w_bf16), row(LANES), row(LANES), full(qg), full(kg), full(ones)],
        out_specs=(row(hgw), row(s5w), row(qkw), row(qkw), pl.BlockSpec((1, vw, tm), lambda i: (i, 0, 0))),
        compiler_params=_cparams(("parallel",)),
        name="in_proj",
    )(x2, gain2, w_bf16, cos, sin, qg, kg, ones)


def _hgrn_kernel(hg_ref, lbraw_ref, gn_ref, mstack_ref, pmask_ref, ones_ref, o_ref, st_ref,
                 *, layer, nchunk, width, heads):
    @pl.when(pl.program_id(0) == 0)
    def _():
        st_ref[...] = jnp.zeros_like(st_ref)

    lbr = lbraw_ref[...]
    e = jnp.exp(lbr - jnp.max(lbr, axis=0, keepdims=True))
    sm = e / jnp.sum(e, axis=0, keepdims=True)
    lb = jnp.zeros((1, width), F32)
    for r in range(1, layer + 1):
        lb = lb + sm[r:r + 1, :]
    lbf = jnp.clip(lb, 0.0, 1.0)
    log_lb = jnp.log(lbf)
    log_1m_lb = jnp.log1p(-lbf)

    row = lax.broadcasted_iota(jnp.int32, (HG_CHUNK, 1), 0)
    gn = gn_ref[...]
    ones = ones_ref[...]

    def chunk(c, carry):
        rows = pl.ds(pl.multiple_of(c * HG_CHUNK, HG_CHUNK), HG_CHUNK)
        q = hg_ref[rows, 0:width]
        fr = hg_ref[rows, width:2 * width]
        vi = hg_ref[rows, 2 * width:3 * width]
        g = hg_ref[rows, 3 * width:4 * width]
        qh = q * jax.nn.sigmoid(q)
        log_sig = jnp.minimum(fr, 0.0) - jnp.log1p(jnp.exp(-jnp.abs(fr)))
        t2 = log_1m_lb + log_sig
        lf = jnp.maximum(log_lb, t2) + jnp.log1p(jnp.exp(-jnp.abs(log_lb - t2)))
        kh = (1.0 - lbf) * jax.nn.sigmoid(-fr)

        hi, lo = _split_bf16(lf)
        r = _dot(mstack_ref[...], jnp.concatenate([hi, lo], axis=1))
        r = r[:, 0:width] + r[:, width:2 * width]
        b = r[0:HG_CHUNK]
        b_last = b[HG_CHUNK - 1:HG_CHUNK, :]

        qls, kls = [], []
        for lv in range(HG_LEVELS):
            hblk = 1 << lv
            w = jnp.exp(-jnp.abs(b - r[(lv + 1) * HG_CHUNK:(lv + 2) * HG_CHUNK]))
            odd = ((row // hblk) % 2) == 1
            qls.append(jnp.where(odd, qh * w, 0.0).astype(BF16))
            kls.append(jnp.where(odd, 0.0, kh * w).astype(BF16))

        o_diag = _group_sum(qh * kh, ones) * vi
        qe = (qh * jnp.exp(b)).astype(BF16)
        kd = (kh * jnp.exp(b_last - b)).astype(BF16)
        dec = jnp.exp(b_last)
        vb = vi.astype(BF16)

        outs = []
        for hd in range(heads):
            hs = slice(hd * HEAD_DIM, (hd + 1) * HEAD_DIM)
            sc = jnp.zeros((HG_CHUNK, HG_CHUNK), F32)
            for lv in range(HG_LEVELS):
                sc = sc + pmask_ref[lv] * _dot_nt(qls[lv][:, hs], kls[lv][:, hs])
            st = st_ref[hd]
            v_h = vb[:, hs]
            o_h = _dot(sc.astype(BF16), v_h) + _dot_nt(qe[:, hs], st.astype(BF16)) + o_diag[:, hs]
            st_ref[hd] = st * dec[:, hs] + _dot_tn(v_h, kd[:, hs])
            ms = jnp.mean(o_h * o_h, axis=-1, keepdims=True)
            outs.append(o_h * lax.rsqrt(ms + EPS) * gn)
        o = jnp.concatenate(outs, axis=1) * (g * jax.nn.sigmoid(g))
        o_ref[rows, :] = o.astype(BF16)
        return carry

    lax.fori_loop(0, nchunk, chunk, 0)


def _hgrn_constants():
    t = np.arange(HG_CHUNK)
    mats = [(t[None, :] <= t[:, None]).astype(np.float32)]
    pmasks = []
    for lv in range(HG_LEVELS):
        hblk = 1 << lv
        boundary = (t // (2 * hblk)) * 2 * hblk + hblk - 1
        mats.append((t[None, :] <= boundary[:, None]).astype(np.float32))
        pmasks.append((t[None, :] // (2 * hblk) == t[:, None] // (2 * hblk)).astype(np.float32))
    return np.concatenate(mats, axis=0), np.stack(pmasks, axis=0)


def _hgrn(hg, lb_raw, gn_gain, *, layer, tt):
    L, w4 = hg.shape
    width = w4 // 4
    heads = width // HEAD_DIM
    mstack_np, pmask_np = _hgrn_constants()
    mstack = jnp.asarray(mstack_np).astype(BF16)
    pmask = jnp.asarray(pmask_np)
    ones = jnp.asarray(np.kron(np.eye(heads), np.ones((HEAD_DIM, HEAD_DIM))).astype(np.float32)).astype(BF16)
    lbr = lb_raw.astype(F32)
    gn = gn_gain.astype(F32)[None, :]
    full = lambda a: pl.BlockSpec(a.shape, lambda i: (0,) * a.ndim)
    return pl.pallas_call(
        functools.partial(_hgrn_kernel, layer=layer, nchunk=tt // HG_CHUNK, width=width, heads=heads),
        out_shape=jax.ShapeDtypeStruct((L, width), BF16),
        grid=(L // tt,),
        in_specs=[pl.BlockSpec((tt, w4), lambda i: (i, 0)), full(lbr), full(gn), full(mstack), full(pmask), full(ones)],
        out_specs=pl.BlockSpec((tt, width), lambda i: (i, 0)),
        scratch_shapes=[pltpu.VMEM((heads, HEAD_DIM, HEAD_DIM), F32)],
        compiler_params=_cparams(("arbitrary",)),
        name="hgrn2",
    )(hg, lbr, gn, mstack, pmask, ones)


def _s5_kernel(u_ref, bblk_ref, cblk_ref, d_ref, tr_ref, ti_ref, pr_ref, pi_ref, wglu_ref, gain_ref,
               o_ref, bu_ref, cr_ref, ci_ref, *, nblk, nstate):
    @pl.when(pl.program_id(0) == 0)
    def _():
        cr_ref[...] = jnp.zeros_like(cr_ref)
        ci_ref[...] = jnp.zeros_like(ci_ref)

    u = u_ref[...]
    bu_ref[...] = _dot(u.astype(BF16), bblk_ref[...])
    pr = pr_ref[...]
    pi = pi_ref[...]

    def blk(i, carry):
        cr, ci = carry
        rows = pl.ds(pl.multiple_of(i * SUBLANES, SUBLANES), SUBLANES)
        xr = bu_ref[rows, 0:nstate]
        xi = bu_ref[rows, nstate:2 * nstate]
        for lv in range(3):
            sh = 1 << lv
            sr = pltpu.roll(xr, sh, 0)
            si = pltpu.roll(xi, sh, 0)
            tr = tr_ref[lv]
            ti = ti_ref[lv]
            xr, xi = xr + (tr * sr - ti * si), xi + (tr * si + ti * sr)
        xr, xi = xr + (pr * cr - pi * ci), xi + (pr * ci + pi * cr)
        bu_ref[rows, 0:nstate] = xr
        bu_ref[rows, nstate:2 * nstate] = xi
        return xr[SUBLANES - 1:SUBLANES, :], xi[SUBLANES - 1:SUBLANES, :]

    cr, ci = lax.fori_loop(0, nblk, blk, (cr_ref[...], ci_ref[...]))
    cr_ref[...] = cr
    ci_ref[...] = ci

    y = _dot(bu_ref[...].astype(BF16), cblk_ref[...]) + d_ref[...] * u
    y = jax.nn.gelu(y)
    z = y * jax.nn.sigmoid(_dot(y.astype(BF16), wglu_ref[...]))
    var = jnp.mean(z * z, axis=-1, keepdims=True)
    o_ref[...] = (z * lax.rsqrt(var + EPS) * gain_ref[...]).astype(BF16)


def _s5(u, a_re, a_im, log_dt, b_re, b_im, c_re, c_im, d_skip, w_glu, out_gain, *, tt):
    L, width = u.shape
    G, P = a_re.shape
    nstate = G * P
    lam_re = jnp.minimum(a_re.astype(F32), -S5_MIN_NEG)
    lam_im = a_im.astype(F32)
    dt = jnp.exp(log_dt.astype(F32))
    mag = jnp.exp(lam_re * dt)
    abar_re, abar_im = mag * jnp.cos(lam_im * dt), mag * jnp.sin(lam_im * dt)
    den = lam_re * lam_re + lam_im * lam_im
    sc_re = ((abar_re - 1.0) * lam_re + abar_im * lam_im) / den
    sc_im = (abar_im * lam_re - (abar_re - 1.0) * lam_im) / den
    bre, bim = b_re.astype(F32), b_im.astype(F32)
    bbar_re = bre * sc_re[..., None] - bim * sc_im[..., None]
    bbar_im = bre * sc_im[..., None] + bim * sc_re[..., None]
    eye = jnp.eye(G, dtype=F32)
    bd = lambda m: jnp.einsum('gph,gk->ghkp', m, eye).reshape(G * S5_GROUP, nstate)
    bblk = jnp.concatenate([bd(bbar_re), bd(bbar_im)], axis=1).astype(BF16)
    cd = lambda m: jnp.einsum('ghp,gk->gpkh', m, eye).reshape(nstate, G * S5_GROUP)
    cblk = jnp.concatenate([cd(c_re.astype(F32)), cd(-c_im.astype(F32))], axis=0).astype(BF16)
    kpow = jnp.arange(1, SUBLANES + 1, dtype=F32)[:, None]
    pmag = jnp.exp((lam_re * dt).reshape(1, nstate) * kpow)
    parg = (lam_im * dt).reshape(1, nstate) * kpow
    p_re, p_im = pmag * jnp.cos(parg), pmag * jnp.sin(parg)
    rowi = np.arange(SUBLANES)[:, None]
    tmask = jnp.asarray(np.stack([(rowi >= (1 << lv)).astype(np.float32) for lv in range(3)]))
    t_re = tmask * jnp.stack([p_re[(1 << lv) - 1] for lv in range(3)])[:, None, :]
    t_im = tmask * jnp.stack([p_im[(1 << lv) - 1] for lv in range(3)])[:, None, :]
    d2 = d_skip.astype(F32).reshape(1, width)
    gain2 = out_gain.astype(F32)[None, :]
    wglu = w_glu.astype(BF16)
    full = lambda a: pl.BlockSpec(a.shape, lambda i: (0,) * a.ndim)
    return pl.pallas_call(
        functools.partial(_s5_kernel, nblk=tt // SUBLANES, nstate=nstate),
        out_shape=jax.ShapeDtypeStruct((L, width), BF16),
        grid=(L // tt,),
        in_specs=[pl.BlockSpec((tt, width), lambda i: (i, 0)), full(bblk), full(cblk), full(d2),
                  full(t_re), full(t_im), full(p_re), full(p_im), full(wglu), full(gain2)],
        out_specs=pl.BlockSpec((tt, width), lambda i: (i, 0)),
        scratch_shapes=[pltpu.VMEM((tt, 2 * nstate), F32), pltpu.VMEM((1, nstate), F32), pltpu.VMEM((1, nstate), F32)],
        compiler_params=_cparams(("arbitrary",)),
        name="s5",
    )(u, bblk, cblk, d2, t_re, t_im, p_re, p_im, wglu, gain2)


def _attn_kernel(posq_ref, posk_ref, q_ref, k_ref, vt_ref, lq1_ref, lk1_ref, lq2_ref, lk2_ref, sg_ref,
                 o_ref, m_ref, l_ref, acc_ref, s_ref, *, tq, lambda_init):
    qi = pl.program_id(1)
    q = q_ref[...]
    lane = lax.broadcasted_iota(jnp.int32, q.shape, 1)
    zero = jnp.zeros_like(q)
    qc = (jnp.where(lane < HEAD_DIM, q, zero), jnp.where(lane < HEAD_DIM, zero, q))

    def scores(j, c):
        kb = k_ref[pl.ds(pl.multiple_of(j * tq, tq), tq), :]
        return _dot_nt(kb, qc[c])

    def update(s, j, c, mask):
        cols = slice(c * tq, (c + 1) * tq)
        if mask is not None:
            s = jnp.where(mask, s, -jnp.inf)
        m_old = m_ref[:, cols]
        m_new = jnp.maximum(m_old, jnp.max(s, axis=0, keepdims=True))
        alpha = jnp.exp(m_old - m_new)
        p = jnp.exp(s - m_new)
        l_ref[:, cols] = alpha * l_ref[:, cols] + jnp.sum(p, axis=0, keepdims=True)
        acc_ref[:, cols] = alpha * acc_ref[:, cols] + _dot(vt_ref[j], p.astype(BF16))
        m_ref[:, cols] = m_new

    m_ref[...] = jnp.full(m_ref.shape, -jnp.inf, F32)
    l_ref[...] = jnp.zeros_like(l_ref)
    acc_ref[...] = jnp.zeros_like(acc_ref)
    s_ref[...] = scores(0, 0)

    def body(j, carry):
        s1 = scores(j, 1)
        update(s_ref[...], j, 0, None)
        s_ref[...] = scores(j + 1, 0)
        update(s1, j, 1, None)
        return carry

    lax.fori_loop(0, qi, body, 0)

    mask = posk_ref[...] <= posq_ref[...]
    s1 = scores(qi, 1)
    update(s_ref[...], qi, 0, mask)
    update(s1, qi, 1, mask)

    lam = (jnp.exp(jnp.sum(lq1_ref[...] * lk1_ref[...], axis=-1, keepdims=True))
           - jnp.exp(jnp.sum(lq2_ref[...] * lk2_ref[...], axis=-1, keepdims=True)) + lambda_init)
    ot = acc_ref[...] / l_ref[...]
    ot = ot[:, 0:tq] - lam * ot[:, tq:2 * tq]
    ms = jnp.mean(ot * ot, axis=0, keepdims=True)
    ot = ot * lax.rsqrt(ms + EPS) * (1.0 - lambda_init)
    o_ref[...] = (jnp.transpose(ot) * sg_ref[...]).astype(BF16)


def _attn(q, k, vt, pos_col, pos_row, lq1, lk1, lq2, lk2, subln_gain, *, lambda_init, tq):
    L, qkw = q.shape
    heads = qkw // (2 * HEAD_DIM)
    nblk = vt.shape[0]
    assert vt.shape[2] == tq and nblk * tq == L
    vec = lambda a: a.astype(F32)[None, :]
    lq1, lk1, lq2, lk2, sg = vec(lq1), vec(lk1), vec(lq2), vec(lk2), vec(subln_gain)
    full = lambda a: pl.BlockSpec(a.shape, lambda h, i: (0,) * a.ndim)
    return pl.pallas_call(
        functools.partial(_attn_kernel, tq=tq, lambda_init=lambda_init),
        out_shape=jax.ShapeDtypeStruct((L, qkw), BF16),
        grid=(heads, L // tq),
        in_specs=[pl.BlockSpec((1, tq), lambda h, i: (0, i)),
                  pl.BlockSpec((tq, 1), lambda h, i: (i, 0)),
                  pl.BlockSpec((tq, LANES), lambda h, i: (i, h)),
                  pl.BlockSpec((L, LANES), lambda h, i: (0, h)),
                  pl.BlockSpec((nblk, LANES, tq), lambda h, i: (0, h, 0)),
                  full(lq1), full(lk1), full(lq2), full(lk2), full(sg)],
        out_specs=pl.BlockSpec((tq, LANES), lambda h, i: (i, h)),
        scratch_shapes=[pltpu.VMEM((1, 2 * tq), F32), pltpu.VMEM((1, 2 * tq), F32), pltpu.VMEM((LANES, 2 * tq), F32),
                        pltpu.VMEM((tq, tq), F32)],
        compiler_params=_cparams(("arbitrary", "arbitrary")),
        name="diff_attn",
    )(pos_row, pos_col, q, k, vt, lq1, lk1, lq2, lk2, sg)


def _mlp_kernel(x_ref, oa_ref, ob_ref, oc_ref, wout_ref, g_ref, wup_ref, wdn_ref, o_ref,
                xres_ref, h_ref, acc_ref, *, wa, wb):
    k = pl.program_id(1)

    @pl.when(k == 0)
    def _():
        xn = (x_ref[...] + _dot(oa_ref[...], wout_ref[0:wa, :]) + _dot(ob_ref[...], wout_ref[wa:wa + wb, :])
              + _dot(oc_ref[...], wout_ref[wa + wb:, :]))
        xres_ref[...] = xn
        var = jnp.mean(xn * xn, axis=-1, keepdims=True)
        h_ref[...] = (xn * lax.rsqrt(var + EPS) * g_ref[...]).astype(BF16)
        acc_ref[...] = jnp.zeros_like(acc_ref)

    up = _dot(h_ref[...], wup_ref[...])
    act = jnp.square(jnp.maximum(up, 0.0))
    acc_ref[...] += _dot(act.astype(BF16), wdn_ref[...])

    @pl.when(k == pl.num_programs(1) - 1)
    def _():
        o_ref[...] = xres_ref[...] + acc_ref[...]


def _mlp(x2, oa, ob, oc, wout, gain, wup, wdn, *, tm, tf):
    L, D = x2.shape
    dff = wup.shape[1]
    wa, wb, wc = oa.shape[1], ob.shape[1], oc.shape[1]
    gain2 = gain.astype(F32)[None, :]
    row = lambda w: pl.BlockSpec((tm, w), lambda i, k: (i, 0))
    return pl.pallas_call(
        functools.partial(_mlp_kernel, wa=wa, wb=wb),
        out_shape=jax.ShapeDtypeStruct((L, D), F32),
        grid=(L // tm, dff // tf),
        in_specs=[row(D), row(wa), row(wb), row(wc),
                  pl.BlockSpec((D, D), lambda i, k: (0, 0)), pl.BlockSpec((1, D), lambda i, k: (0, 0)),
                  pl.BlockSpec((D, tf), lambda i, k: (0, k)), pl.BlockSpec((tf, D), lambda i, k: (k, 0))],
        out_specs=row(D),
        scratch_shapes=[pltpu.VMEM((tm, D), F32), pltpu.VMEM((tm, D), BF16), pltpu.VMEM((tm, D), F32)],
        compiler_params=_cparams(("parallel", "arbitrary")),
        name="out_proj_mlp",
    )(x2, oa, ob, oc, wout, gain2, wup, wdn)


def kernel(x, positions, hgrn_lower_bounds, attn_norm_gain, w_in, hg_norm_gain, s5_a_re, s5_a_im, s5_log_dt, s5_b_re, s5_b_im, s5_c_re, s5_c_im, s5_d, s5_w_glu, s5_norm_gain, da_q_norm_gain, da_k_norm_gain, da_lambda_q1, da_lambda_k1, da_lambda_q2, da_lambda_k2, da_subln_gain, w_out, mlp_norm_gain, w_mlp_up, w_mlp_down):
    batch, L, D = x.shape
    assert batch == 1
    depth = w_in.shape[0]
    tile = min(L, 512)
    x2 = x.reshape(L, D)
    pos_col = positions.reshape(L, 1)
    pos_row = positions.reshape(1, L)
    cos, sin = _rope_tables(pos_col)
    for l in range(depth):
        lambda_init = 0.8 - 0.6 * math.exp(-0.3 * l)
        hg, su, q, k, v = _proj(x2, attn_norm_gain[l], w_in[l].astype(BF16), cos, sin,
                                da_q_norm_gain[l], da_k_norm_gain[l], tm=tile)
        o_a = _hgrn(hg, hgrn_lower_bounds, hg_norm_gain[l], layer=l, tt=tile)
        o_b = _s5(su, s5_a_re[l], s5_a_im[l], s5_log_dt[l], s5_b_re[l], s5_b_im[l], s5_c_re[l], s5_c_im[l],
                  s5_d[l], s5_w_glu[l], s5_norm_gain[l], tt=tile)
        o_c = _attn(q, k, v, pos_col, pos_row, da_lambda_q1[l], da_lambda_k1[l], da_lambda_q2[l],
                    da_lambda_k2[l], da_subln_gain[l], lambda_init=lambda_init, tq=tile)
        x2 = _mlp(x2, o_a, o_b, o_c, w_out[l].astype(BF16), mlp_norm_gain[l],
                  w_mlp_up[l].astype(BF16), w_mlp_down[l].astype(BF16), tm=tile, tf=1024)
    return x2.reshape(batch, L, D)
```

```python
import functools
import math

import numpy as np
import jax
import jax.numpy as jnp
from jax import lax
from jax.experimental import pallas as pl
from jax.experimental.pallas import tpu as pltpu

F32 = jnp.float32
BF16 = jnp.bfloat16

EPS = 1e-6
ROPE_THETA = 10000.0
S5_MIN_NEG = 1e-4
LOG2_E = math.log2(math.e)

HEAD_DIM = 64
HG_CHUNK = 128
HG_LEVELS = 7
S5_GROUP = 16
S5_STATE = 64
SUBLANES = 8
LANES = 128

VMEM_LIMIT = 56 * 1024 * 1024


def _cparams(semantics):
    return pltpu.CompilerParams(dimension_semantics=semantics, vmem_limit_bytes=VMEM_LIMIT)


def _dot(a, b):
    return jnp.dot(a, b, preferred_element_type=F32)


def _dot_nt(a, b):
    return lax.dot_general(a, b, (((1,), (1,)), ((), ())), preferred_element_type=F32)


def _dot_tn(a, b):
    return lax.dot_general(a, b, (((0,), (0,)), ((), ())), preferred_element_type=F32)


def _split_bf16(t):
    hi = t.astype(BF16)
    lo = (t - hi.astype(F32)).astype(BF16)
    return hi, lo


def _group_sum(t, ones_blockdiag):
    hi, lo = _split_bf16(t)
    return _dot(hi, ones_blockdiag) + _dot(lo, ones_blockdiag)


def _rope_kernel(pos_ref, invf_ref, cos_ref, sin_ref):
    ang = pos_ref[...].astype(F32) * invf_ref[...]
    lane = lax.broadcasted_iota(jnp.int32, ang.shape, 1)
    first_half = (lane % HEAD_DIM) < (HEAD_DIM // 2)
    s = jnp.sin(ang)
    cos_ref[...] = jnp.cos(ang)
    sin_ref[...] = jnp.where(first_half, -s, s)


def _rope_tables(pos_col):
    L = pos_col.shape[0]
    tr = min(L, 2048)
    inv_freq = ROPE_THETA ** (-jnp.arange(0, HEAD_DIM, 2, dtype=F32) / HEAD_DIM)
    invf = jnp.tile(inv_freq, LANES // (HEAD_DIM // 2))[None, :]
    return pl.pallas_call(
        _rope_kernel,
        out_shape=(jax.ShapeDtypeStruct((L, LANES), F32), jax.ShapeDtypeStruct((L, LANES), F32)),
        grid=(L // tr,),
        in_specs=[pl.BlockSpec((tr, 1), lambda i: (i, 0)), pl.BlockSpec((1, LANES), lambda i: (0, 0))],
        out_specs=(pl.BlockSpec((tr, LANES), lambda i: (i, 0)), pl.BlockSpec((tr, LANES), lambda i: (i, 0))),
        compiler_params=_cparams(("parallel",)),
        name="rope_tables",
    )(pos_col, invf)


def _proj_kernel(x_ref, g_ref, w_ref, cos_ref, sin_ref, qg_ref, kg_ref, ones_ref,
                 hg_ref, su_ref, q_ref, k_ref, v_ref, *, splits):
    x = x_ref[...]
    var = jnp.mean(x * x, axis=-1, keepdims=True)
    h = (x * lax.rsqrt(var + EPS) * g_ref[...]).astype(BF16)
    c0, c1, c2, c3, c4, c5 = splits
    hg_ref[...] = _dot(h, w_ref[:, c0:c1])
    su_ref[...] = _dot(h, w_ref[:, c1:c2])
    v_ref[0] = jnp.transpose(_dot(h, w_ref[:, c4:c5])).astype(BF16)

    cos = cos_ref[...]
    sin = sin_ref[...]
    lane = lax.broadcasted_iota(jnp.int32, cos.shape, 1)
    first_half = (lane % HEAD_DIM) < (HEAD_DIM // 2)
    ones = ones_ref[...]

    def norm_rope(t, gain, scale):
        ms = _group_sum(t * t, ones) * (1.0 / HEAD_DIM)
        tn = t * lax.rsqrt(ms + EPS) * gain
        outs = []
        for j in range(t.shape[1] // LANES):
            tj = tn[:, j * LANES:(j + 1) * LANES]
            partner = jnp.where(first_half,
                                pltpu.roll(tj, LANES - HEAD_DIM // 2, 1),
                                pltpu.roll(tj, HEAD_DIM // 2, 1))
            outs.append((tj * cos + partner * sin) * scale)
        return jnp.concatenate(outs, axis=1)

    q_ref[...] = norm_rope(_dot(h, w_ref[:, c2:c3]), qg_ref[...], LOG2_E * HEAD_DIM ** -0.5).astype(BF16)
    k_ref[...] = norm_rope(_dot(h, w_ref[:, c3:c4]), kg_ref[...], 1.0).astype(BF16)


def _proj(x2, gain, w_bf16, cos, sin, q_gain, k_gain, *, tm):
    L, D = x2.shape
    ncols = w_bf16.shape[1]
    hgw, s5w, qkw, vw = 4 * (D // 4), D // 4, D // 2, D // 2
    splits = (0, hgw, hgw + s5w, hgw + s5w + qkw, hgw + s5w + 2 * qkw, hgw + s5w + 2 * qkw + vw)
    assert splits[-1] == ncols
    ones = jnp.asarray(np.kron(np.eye(qkw // HEAD_DIM), np.ones((HEAD_DIM, HEAD_DIM))).astype(np.float32)).astype(BF16)
    qg = jnp.tile(q_gain.astype(F32), qkw // HEAD_DIM)[None, :]
    kg = jnp.tile(k_gain.astype(F32), qkw // HEAD_DIM)[None, :]
    row = lambda w: pl.BlockSpec((tm, w), lambda i: (i, 0))
    full = lambda a: pl.BlockSpec(a.shape, lambda i: (0,) * a.ndim)
    gain2 = gain.astype(F32)[None, :]
    return pl.pallas_call(
        functools.partial(_proj_kernel, splits=splits),
        out_shape=(jax.ShapeDtypeStruct((L, hgw), F32), jax.ShapeDtypeStruct((L, s5w), F32),
                   jax.ShapeDtypeStruct((L, qkw), BF16), jax.ShapeDtypeStruct((L, qkw), BF16),
                   jax.ShapeDtypeStruct((L // tm, vw, tm), BF16)),
        grid=(L // tm,),
        in_specs=[row(D), full(gain2), full(w_bf16), row(LANES), row(LANES), full(qg), full(kg), full(ones)],
        out_specs=(row(hgw), row(s5w), row(qkw), row(qkw), pl.BlockSpec((1, vw, tm), lambda i: (i, 0, 0))),
        compiler_params=_cparams(("parallel",)),
        name="in_proj",
    )(x2, gain2, w_bf16, cos, sin, qg, kg, ones)


def _hgrn_kernel(hg_ref, lbraw_ref, gn_ref, mstack_ref, pmask_ref, ones_ref, o_ref, st_ref,
                 *, layer, nchunk, width, heads):
    @pl.when(pl.program_id(0) == 0)
    def _():
        st_ref[...] = jnp.zeros_like(st_ref)

    lbr = lbraw_ref[...]
    e = jnp.exp(lbr - jnp.max(lbr, axis=0, keepdims=True))
    sm = e / jnp.sum(e, axis=0, keepdims=True)
    lb = jnp.zeros((1, width), F32)
    for r in range(1, layer + 1):
        lb = lb + sm[r:r + 1, :]
    lbf = jnp.clip(lb, 0.0, 1.0)
    log_lb = jnp.log(lbf)
    log_1m_lb = jnp.log1p(-lbf)

    row = lax.broadcasted_iota(jnp.int32, (HG_CHUNK, 1), 0)
    gn = gn_ref[...]
    ones = ones_ref[...]

    def chunk(c, carry):
        rows = pl.ds(pl.multiple_of(c * HG_CHUNK, HG_CHUNK), HG_CHUNK)
        q = hg_ref[rows, 0:width]
        fr = hg_ref[rows, width:2 * width]
        vi = hg_ref[rows, 2 * width:3 * width]
        g = hg_ref[rows, 3 * width:4 * width]
        qh = q * jax.nn.sigmoid(q)
        log_sig = jnp.minimum(fr, 0.0) - jnp.log1p(jnp.exp(-jnp.abs(fr)))
        t2 = log_1m_lb + log_sig
        lf = jnp.maximum(log_lb, t2) + jnp.log1p(jnp.exp(-jnp.abs(log_lb - t2)))
        kh = (1.0 - lbf) * jax.nn.sigmoid(-fr)

        hi, lo = _split_bf16(lf)
        r = _dot(mstack_ref[...], jnp.concatenate([hi, lo], axis=1))
        r = r[:, 0:width] + r[:, width:2 * width]
        b = r[0:HG_CHUNK]
        b_last = b[HG_CHUNK - 1:HG_CHUNK, :]

        qls, kls = [], []
        for lv in range(HG_LEVELS):
            hblk = 1 << lv
            w = jnp.exp(-jnp.abs(b - r[(lv + 1) * HG_CHUNK:(lv + 2) * HG_CHUNK]))
            odd = ((row // hblk) % 2) == 1
            qls.append(jnp.where(odd, qh * w, 0.0).astype(BF16))
            kls.append(jnp.where(odd, 0.0, kh * w).astype(BF16))

        o_diag = _group_sum(qh * kh, ones) * vi
        qe = (qh * jnp.exp(b)).astype(BF16)
        kd = (kh * jnp.exp(b_last - b)).astype(BF16)
        dec = jnp.exp(b_last)
        vb = vi.astype(BF16)

        outs = []
        for hd in range(heads):
            hs = slice(hd * HEAD_DIM, (hd + 1) * HEAD_DIM)
            sc = jnp.zeros((HG_CHUNK, HG_CHUNK), F32)
            for lv in range(HG_LEVELS):
                sc = sc + pmask_ref[lv] * _dot_nt(qls[lv][:, hs], kls[lv][:, hs])
            st = st_ref[hd]
            v_h = vb[:, hs]
            o_h = _dot(sc.astype(BF16), v_h) + _dot_nt(qe[:, hs], st.astype(BF16)) + o_diag[:, hs]
            st_ref[hd] = st * dec[:, hs] + _dot_tn(v_h, kd[:, hs])
            ms = jnp.mean(o_h * o_h, axis=-1, keepdims=True)
            outs.append(o_h * lax.rsqrt(ms + EPS) * gn)
        o = jnp.concatenate(outs, axis=1) * (g * jax.nn.sigmoid(g))
        o_ref[rows, :] = o.astype(BF16)
        return carry

    lax.fori_loop(0, nchunk, chunk, 0)


def _hgrn_constants():
    t = np.arange(HG_CHUNK)
    mats = [(t[None, :] <= t[:, None]).astype(np.float32)]
    pmasks = []
    for lv in range(HG_LEVELS):
        hblk = 1 << lv
        boundary = (t // (2 * hblk)) * 2 * hblk + hblk - 1
        mats.append((t[None, :] <= boundary[:, None]).astype(np.float32))
        pmasks.append((t[None, :] // (2 * hblk) == t[:, None] // (2 * hblk)).astype(np.float32))
    return np.concatenate(mats, axis=0), np.stack(pmasks, axis=0)


def _hgrn(hg, lb_raw, gn_gain, *, layer, tt):
    L, w4 = hg.shape
    width = w4 // 4
    heads = width // HEAD_DIM
    mstack_np, pmask_np = _hgrn_constants()
    mstack = jnp.asarray(mstack_np).astype(BF16)
    pmask = jnp.asarray(pmask_np)
    ones = jnp.asarray(np.kron(np.eye(heads), np.ones((HEAD_DIM, HEAD_DIM))).astype(np.float32)).astype(BF16)
    lbr = lb_raw.astype(F32)
    gn = gn_gain.astype(F32)[None, :]
    full = lambda a: pl.BlockSpec(a.shape, lambda i: (0,) * a.ndim)
    return pl.pallas_call(
        functools.partial(_hgrn_kernel, layer=layer, nchunk=tt // HG_CHUNK, width=width, heads=heads),
        out_shape=jax.ShapeDtypeStruct((L, width), BF16),
        grid=(L // tt,),
        in_specs=[pl.BlockSpec((tt, w4), lambda i: (i, 0)), full(lbr), full(gn), full(mstack), full(pmask), full(ones)],
        out_specs=pl.BlockSpec((tt, width), lambda i: (i, 0)),
        scratch_shapes=[pltpu.VMEM((heads, HEAD_DIM, HEAD_DIM), F32)],
        compiler_params=_cparams(("arbitrary",)),
        name="hgrn2",
    )(hg, lbr, gn, mstack, pmask, ones)


def _s5_kernel(u_ref, bblk_ref, cblk_ref, d_ref, tr_ref, ti_ref, pr_ref, pi_ref, wglu_ref, gain_ref,
               o_ref, bu_ref, cr_ref, ci_ref, *, nblk, nstate):
    @pl.when(pl.program_id(0) == 0)
    def _():
        cr_ref[...] = jnp.zeros_like(cr_ref)
        ci_ref[...] = jnp.zeros_like(ci_ref)

    u = u_ref[...]
    bu_ref[...] = _dot(u.astype(BF16), bblk_ref[...])
    pr = pr_ref[...]
    pi = pi_ref[...]

    def blk(i, carry):
        cr, ci = carry
        rows = pl.ds(pl.multiple_of(i * SUBLANES, SUBLANES), SUBLANES)
        xr = bu_ref[rows, 0:nstate]
        xi = bu_ref[rows, nstate:2 * nstate]
        for lv in range(3):
            sh = 1 << lv
            sr = pltpu.roll(xr, sh, 0)
            si = pltpu.roll(xi, sh, 0)
            tr = tr_ref[lv]
            ti = ti_ref[lv]
            xr, xi = xr + (tr * sr - ti * si), xi + (tr * si + ti * sr)
        xr, xi = xr + (pr * cr - pi * ci), xi + (pr * ci + pi * cr)
        bu_ref[rows, 0:nstate] = xr
        bu_ref[rows, nstate:2 * nstate] = xi
        return xr[SUBLANES - 1:SUBLANES, :], xi[SUBLANES - 1:SUBLANES, :]

    cr, ci = lax.fori_loop(0, nblk, blk, (cr_ref[...], ci_ref[...]))
    cr_ref[...] = cr
    ci_ref[...] = ci

    y = _dot(bu_ref[...].astype(BF16), cblk_ref[...]) + d_ref[...] * u
    y = jax.nn.gelu(y)
    z = y * jax.nn.sigmoid(_dot(y.astype(BF16), wglu_ref[...]))
    var = jnp.mean(z * z, axis=-1, keepdims=True)
    o_ref[...] = (z * lax.rsqrt(var + EPS) * gain_ref[...]).astype(BF16)


def _s5(u, a_re, a_im, log_dt, b_re, b_im, c_re, c_im, d_skip, w_glu, out_gain, *, tt):
    L, width = u.shape
    G, P = a_re.shape
    nstate = G * P
    lam_re = jnp.minimum(a_re.astype(F32), -S5_MIN_NEG)
    lam_im = a_im.astype(F32)
    dt = jnp.exp(log_dt.astype(F32))
    mag = jnp.exp(lam_re * dt)
    abar_re, abar_im = mag * jnp.cos(lam_im * dt), mag * jnp.sin(lam_im * dt)
    den = lam_re * lam_re + lam_im * lam_im
    sc_re = ((abar_re - 1.0) * lam_re + abar_im * lam_im) / den
    sc_im = (abar_im * lam_re - (abar_re - 1.0) * lam_im) / den
    bre, bim = b_re.astype(F32), b_im.astype(F32)
    bbar_re = bre * sc_re[..., None] - bim * sc_im[..., None]
    bbar_im = bre * sc_im[..., None] + bim * sc_re[..., None]
    eye = jnp.eye(G, dtype=F32)
    bd = lambda m: jnp.einsum('gph,gk->ghkp', m, eye).reshape(G * S5_GROUP, nstate)
    bblk = jnp.concatenate([bd(bbar_re), bd(bbar_im)], axis=1).astype(BF16)
    cd = lambda m: jnp.einsum('ghp,gk->gpkh', m, eye).reshape(nstate, G * S5_GROUP)
    cblk = jnp.concatenate([cd(c_re.astype(F32)), cd(-c_im.astype(F32))], axis=0).astype(BF16)
    kpow = jnp.arange(1, SUBLANES + 1, dtype=F32)[:, None]
    pmag = jnp.exp((lam_re * dt).reshape(1, nstate) * kpow)
    parg = (lam_im * dt).reshape(1, nstate) * kpow
    p_re, p_im = pmag * jnp.cos(parg), pmag * jnp.sin(parg)
    rowi = np.arange(SUBLANES)[:, None]
    tmask = jnp.asarray(np.stack([(rowi >= (1 << lv)).astype(np.float32) for lv in range(3)]))
    t_re = tmask * jnp.stack([p_re[(1 << lv) - 1] for lv in range(3)])[:, None, :]
    t_im = tmask * jnp.stack([p_im[(1 << lv) - 1] for lv in range(3)])[:, None, :]
    d2 = d_skip.astype(F32).reshape(1, width)
    gain2 = out_gain.astype(F32)[None, :]
    wglu = w_glu.astype(BF16)
    full = lambda a: pl.BlockSpec(a.shape, lambda i: (0,) * a.ndim)
    return pl.pallas_call(
        functools.partial(_s5_kernel, nblk=tt // SUBLANES, nstate=nstate),
        out_shape=jax.ShapeDtypeStruct((L, width), BF16),
        grid=(L // tt,),
        in_specs=[pl.BlockSpec((tt, width), lambda i: (i, 0)), full(bblk), full(cblk), full(d2),
                  full(t_re), full(t_im), full(p_re), full(p_im), full(wglu), full(gain2)],
        out_specs=pl.BlockSpec((tt, width), lambda i: (i, 0)),
        scratch_shapes=[pltpu.VMEM((tt, 2 * nstate), F32), pltpu.VMEM((1, nstate), F32), pltpu.VMEM((1, nstate), F32)],
        compiler_params=_cparams(("arbitrary",)),
        name="s5",
    )(u, bblk, cblk, d2, t_re, t_im, p_re, p_im, wglu, gain2)


def _attn_kernel(posq_ref, posk_ref, q_ref, k_ref, vt_ref, lq1_ref, lk1_ref, lq2_ref, lk2_ref, sg_ref,
                 o_ref, m_ref, l_ref, acc_ref, sa_ref, sb_ref, mba_ref, mbb_ref, p_ref, *, tq, lambda_init):
    qi = pl.program_id(1)
    q = q_ref[...]
    lane = lax.broadcasted_iota(jnp.int32, q.shape, 1)
    zero = jnp.zeros_like(q)
    q2 = jnp.concatenate([jnp.where(lane < HEAD_DIM, q, zero), jnp.where(lane < HEAD_DIM, zero, q)], axis=0)

    def produce(j, s_dst, mb_dst):
        kb = k_ref[pl.ds(pl.multiple_of(j * tq, tq), tq), :]
        sc = _dot_nt(kb, q2)
        s_dst[...] = sc
        mb_dst[...] = jnp.max(sc, axis=0, keepdims=True)

    def step(j, src, dst, mask):
        s_src, mb_src = src
        pv = _dot(vt_ref[jnp.maximum(j - 1, 0)], p_ref[...])
        if dst is not None:
            produce(j + 1, *dst)
        s = s_src[...]
        if mask is None:
            m_blk = mb_src[...]
        else:
            s = jnp.where(jnp.concatenate([mask, mask], axis=1), s, -jnp.inf)
            m_blk = jnp.max(s, axis=0, keepdims=True)
        m_old = m_ref[...]
        m_new = jnp.maximum(m_old, m_blk)
        alpha = jnp.exp2(m_old - m_new)
        p = jnp.exp2(s - m_new)
        l_ref[...] = alpha * l_ref[...] + jnp.sum(p, axis=0, keepdims=True)
        m_ref[...] = m_new
        acc_ref[...] = alpha * (acc_ref[...] + pv)
        p_ref[...] = p.astype(BF16)

    buf_a = (sa_ref, mba_ref)
    buf_b = (sb_ref, mbb_ref)
    m_ref[...] = jnp.full(m_ref.shape, -jnp.inf, F32)
    l_ref[...] = jnp.zeros_like(l_ref)
    acc_ref[...] = jnp.zeros_like(acc_ref)
    p_ref[...] = jnp.zeros_like(p_ref)
    produce(0, *buf_a)

    def body(t, carry):
        step(2 * t, buf_a, buf_b, None)
        step(2 * t + 1, buf_b, buf_a, None)
        return carry

    lax.fori_loop(0, qi // 2, body, 0)

    mask = posk_ref[...] <= posq_ref[...]

    @pl.when(qi % 2 == 0)
    def _():
        step(qi, buf_a, None, mask)

    @pl.when(qi % 2 == 1)
    def _():
        step(qi - 1, buf_a, buf_b, None)
        step(qi, buf_b, None, mask)

    acc_ref[...] = acc_ref[...] + _dot(vt_ref[qi], p_ref[...])

    lam = (jnp.exp(jnp.sum(lq1_ref[...] * lk1_ref[...], axis=-1, keepdims=True))
           - jnp.exp(jnp.sum(lq2_ref[...] * lk2_ref[...], axis=-1, keepdims=True)) + lambda_init)
    ot = acc_ref[...] / l_ref[...]
    ot = ot[:, 0:tq] - lam * ot[:, tq:2 * tq]
    ms = jnp.mean(ot * ot, axis=0, keepdims=True)
    ot = ot * lax.rsqrt(ms + EPS) * (1.0 - lambda_init)
    o_ref[...] = (jnp.transpose(ot) * sg_ref[...]).astype(BF16)


def _attn(q, k, vt, pos_col, pos_row, lq1, lk1, lq2, lk2, subln_gain, *, lambda_init, tq):
    L, qkw = q.shape
    heads = qkw // (2 * HEAD_DIM)
    nblk = vt.shape[0]
    assert vt.shape[2] == tq and nblk * tq == L
    vec = lambda a: a.astype(F32)[None, :]
    lq1, lk1, lq2, lk2, sg = vec(lq1), vec(lk1), vec(lq2), vec(lk2), vec(subln_gain)
    full = lambda a: pl.BlockSpec(a.shape, lambda h, i: (0,) * a.ndim)
    return pl.pallas_call(
        functools.partial(_attn_kernel, tq=tq, lambda_init=lambda_init),
        out_shape=jax.ShapeDtypeStruct((L, qkw), BF16),
        grid=(heads, L // tq),
        in_specs=[pl.BlockSpec((1, tq), lambda h, i: (0, i)),
                  pl.BlockSpec((tq, 1), lambda h, i: (i, 0)),
                  pl.BlockSpec((tq, LANES), lambda h, i: (i, h)),
                  pl.BlockSpec((L, LANES), lambda h, i: (0, h)),
                  pl.BlockSpec((nblk, LANES, tq), lambda h, i: (0, h, 0)),
                  full(lq1), full(lk1), full(lq2), full(lk2), full(sg)],
        out_specs=pl.BlockSpec((tq, LANES), lambda h, i: (i, h)),
        scratch_shapes=[pltpu.VMEM((1, 2 * tq), F32), pltpu.VMEM((1, 2 * tq), F32), pltpu.VMEM((LANES, 2 * tq), F32),
                        pltpu.VMEM((tq, 2 * tq), F32), pltpu.VMEM((tq, 2 * tq), F32),
                        pltpu.VMEM((1, 2 * tq), F32), pltpu.VMEM((1, 2 * tq), F32), pltpu.VMEM((tq, 2 * tq), BF16)],
        compiler_params=_cparams(("arbitrary", "arbitrary")),
        name="diff_attn",
    )(pos_row, pos_col, q, k, vt, lq1, lk1, lq2, lk2, sg)


def _mlp_kernel(x_ref, oa_ref, ob_ref, oc_ref, wout_ref, g_ref, wup_ref, wdn_ref, o_ref,
                xres_ref, h_ref, acc_ref, *, wa, wb):
    k = pl.program_id(1)

    @pl.when(k == 0)
    def _():
        xn = (x_ref[...] + _dot(oa_ref[...], wout_ref[0:wa, :]) + _dot(ob_ref[...], wout_ref[wa:wa + wb, :])
              + _dot(oc_ref[...], wout_ref[wa + wb:, :]))
        xres_ref[...] = xn
        var = jnp.mean(xn * xn, axis=-1, keepdims=True)
        h_ref[...] = (xn * lax.rsqrt(var + EPS) * g_ref[...]).astype(BF16)
        acc_ref[...] = jnp.zeros_like(acc_ref)

    up = _dot(h_ref[...], wup_ref[...])
    act = jnp.square(jnp.maximum(up, 0.0))
    acc_ref[...] += _dot(act.astype(BF16), wdn_ref[...])

    @pl.when(k == pl.num_programs(1) - 1)
    def _():
        o_ref[...] = xres_ref[...] + acc_ref[...]


def _mlp(x2, oa, ob, oc, wout, gain, wup, wdn, *, tm, tf):
    L, D = x2.shape
    dff = wup.shape[1]
    wa, wb, wc = oa.shape[1], ob.shape[1], oc.shape[1]
    gain2 = gain.astype(F32)[None, :]
    row = lambda w: pl.BlockSpec((tm, w), lambda i, k: (i, 0))
    return pl.pallas_call(
        functools.partial(_mlp_kernel, wa=wa, wb=wb),
        out_shape=jax.ShapeDtypeStruct((L, D), F32),
        grid=(L // tm, dff // tf),
        in_specs=[row(D), row(wa), row(wb), row(wc),
                  pl.BlockSpec((D, D), lambda i, k: (0, 0)), pl.BlockSpec((1, D), lambda i, k: (0, 0)),
                  pl.BlockSpec((D, tf), lambda i, k: (0, k)), pl.BlockSpec((tf, D), lambda i, k: (k, 0))],
        out_specs=row(D),
        scratch_shapes=[pltpu.VMEM((tm, D), F32), pltpu.VMEM((tm, D), BF16), pltpu.VMEM((tm, D), F32)],
        compiler_params=_cparams(("parallel", "arbitrary")),
        name="out_proj_mlp",
    )(x2, oa, ob, oc, wout, gain2, wup, wdn)


def kernel(x, positions, hgrn_lower_bounds, attn_norm_gain, w_in, hg_norm_gain, s5_a_re, s5_a_im, s5_log_dt, s5_b_re, s5_b_im, s5_c_re, s5_c_im, s5_d, s5_w_glu, s5_norm_gain, da_q_norm_gain, da_k_norm_gain, da_lambda_q1, da_lambda_k1, da_lambda_q2, da_lambda_k2, da_subln_gain, w_out, mlp_norm_gain, w_mlp_up, w_mlp_down):
    batch, L, D = x.shape
    assert batch == 1
    depth = w_in.shape[0]
    tile = min(L, 512)
    x2 = x.reshape(L, D)
    pos_col = positions.reshape(L, 1)
    pos_row = positions.reshape(1, L)
    cos, sin = _rope_tables(pos_col)
    for l in range(depth):
        lambda_init = 0.8 - 0.6 * math.exp(-0.3 * l)
        hg, su, q, k, v = _proj(x2, attn_norm_gain[l], w_in[l].astype(BF16), cos, sin,
                                da_q_norm_gain[l], da_k_norm_gain[l], tm=tile)
        o_a = _hgrn(hg, hgrn_lower_bounds, hg_norm_gain[l], layer=l, tt=tile)
        o_b = _s5(su, s5_a_re[l], s5_a_im[l], s5_log_dt[l], s5_b_re[l], s5_b_im[l], s5_c_re[l], s5_c_im[l],
                  s5_d[l], s5_w_glu[l], s5_norm_gain[l], tt=tile)
        o_c = _attn(q, k, v, pos_col, pos_row, da_lambda_q1[l], da_lambda_k1[l], da_lambda_q2[l],
                    da_lambda_k2[l], da_subln_gain[l], lambda_init=lambda_init, tq=tile)
        x2 = _mlp(x2, o_a, o_b, o_c, w_out[l].astype(BF16), mlp_norm_gain[l],
                  w_mlp_up[l].astype(BF16), w_mlp_down[l].astype(BF16), tm=tile, tf=1024)
    return x2.reshape(batch, L, D)
```

```python
import functools
import math

import numpy as np
import jax
import jax.numpy as jnp
from jax import lax
from jax.experimental import pallas as pl
from jax.experimental.pallas import tpu as pltpu

F32 = jnp.float32
BF16 = jnp.bfloat16

EPS = 1e-6
ROPE_THETA = 10000.0
S5_MIN_NEG = 1e-4
LOG2_E = math.log2(math.e)

HEAD_DIM = 64
HG_CHUNK = 128
HG_LEVELS = 7
S5_GROUP = 16
S5_STATE = 64
SUBLANES = 8
LANES = 128

VMEM_LIMIT = 56 * 1024 * 1024


def _cparams(semantics):
    return pltpu.CompilerParams(dimension_semantics=semantics, vmem_limit_bytes=VMEM_LIMIT)


def _dot(a, b):
    return jnp.dot(a, b, preferred_element_type=F32)


def _dot_nt(a, b):
    return lax.dot_general(a, b, (((1,), (1,)), ((), ())), preferred_element_type=F32)


def _dot_tn(a, b):
    return lax.dot_general(a, b, (((0,), (0,)), ((), ())), preferred_element_type=F32)


def _split_bf16(t):
    hi = t.astype(BF16)
    lo = (t - hi.astype(F32)).astype(BF16)
    return hi, lo


def _group_sum(t, ones_blockdiag):
    hi, lo = _split_bf16(t)
    return _dot(hi, ones_blockdiag) + _dot(lo, ones_blockdiag)


def _rope_kernel(pos_ref, invf_ref, cos_ref, sin_ref):
    ang = pos_ref[...].astype(F32) * invf_ref[...]
    lane = lax.broadcasted_iota(jnp.int32, ang.shape, 1)
    first_half = (lane % HEAD_DIM) < (HEAD_DIM // 2)
    s = jnp.sin(ang)
    cos_ref[...] = jnp.cos(ang)
    sin_ref[...] = jnp.where(first_half, -s, s)


def _rope_tables(pos_col):
    L = pos_col.shape[0]
    tr = min(L, 2048)
    inv_freq = ROPE_THETA ** (-jnp.arange(0, HEAD_DIM, 2, dtype=F32) / HEAD_DIM)
    invf = jnp.tile(inv_freq, LANES // (HEAD_DIM // 2))[None, :]
    return pl.pallas_call(
        _rope_kernel,
        out_shape=(jax.ShapeDtypeStruct((L, LANES), F32), jax.ShapeDtypeStruct((L, LANES), F32)),
        grid=(L // tr,),
        in_specs=[pl.BlockSpec((tr, 1), lambda i: (i, 0)), pl.BlockSpec((1, LANES), lambda i: (0, 0))],
        out_specs=(pl.BlockSpec((tr, LANES), lambda i: (i, 0)), pl.BlockSpec((tr, LANES), lambda i: (i, 0))),
        compiler_params=_cparams(("parallel",)),
        name="rope_tables",
    )(pos_col, invf)


def _proj_kernel(x_ref, g_ref, w_ref, cos_ref, sin_ref, qg_ref, kg_ref, ones_ref,
                 hg_ref, su_ref, q_ref, k_ref, v_ref, *, splits):
    x = x_ref[...]
    var = jnp.mean(x * x, axis=-1, keepdims=True)
    h = (x * lax.rsqrt(var + EPS) * g_ref[...]).astype(BF16)
    c0, c1, c2, c3, c4, c5 = splits
    hg_ref[...] = _dot(h, w_ref[:, c0:c1])
    su_ref[...] = _dot(h, w_ref[:, c1:c2])
    v_ref[0] = jnp.transpose(_dot(h, w_ref[:, c4:c5])).astype(BF16)

    cos = cos_ref[...]
    sin = sin_ref[...]
    lane = lax.broadcasted_iota(jnp.int32, cos.shape, 1)
    first_half = (lane % HEAD_DIM) < (HEAD_DIM // 2)
    ones = ones_ref[...]

    def norm_rope(t, gain, scale):
        ms = _group_sum(t * t, ones) * (1.0 / HEAD_DIM)
        tn = t * lax.rsqrt(ms + EPS) * gain
        outs = []
        for j in range(t.shape[1] // LANES):
            tj = tn[:, j * LANES:(j + 1) * LANES]
            partner = jnp.where(first_half,
                                pltpu.roll(tj, LANES - HEAD_DIM // 2, 1),
                                pltpu.roll(tj, HEAD_DIM // 2, 1))
            outs.append((tj * cos + partner * sin) * scale)
        return jnp.concatenate(outs, axis=1)

    q = norm_rope(_dot(h, w_ref[:, c2:c3]), qg_ref[...], LOG2_E * HEAD_DIM ** -0.5)
    q_ref[0] = jnp.transpose(q).astype(BF16)
    k_ref[...] = norm_rope(_dot(h, w_ref[:, c3:c4]), kg_ref[...], 1.0).astype(BF16)


def _proj(x2, gain, w_bf16, cos, sin, q_gain, k_gain, *, tm):
    L, D = x2.shape
    ncols = w_bf16.shape[1]
    hgw, s5w, qkw, vw = 4 * (D // 4), D // 4, D // 2, D // 2
    splits = (0, hgw, hgw + s5w, hgw + s5w + qkw, hgw + s5w + 2 * qkw, hgw + s5w + 2 * qkw + vw)
    assert splits[-1] == ncols
    ones = jnp.asarray(np.kron(np.eye(qkw // HEAD_DIM), np.ones((HEAD_DIM, HEAD_DIM))).astype(np.float32)).astype(BF16)
    qg = jnp.tile(q_gain.astype(F32), qkw // HEAD_DIM)[None, :]
    kg = jnp.tile(k_gain.astype(F32), qkw // HEAD_DIM)[None, :]
    row = lambda w: pl.BlockSpec((tm, w), lambda i: (i, 0))
    full = lambda a: pl.BlockSpec(a.shape, lambda i: (0,) * a.ndim)
    gain2 = gain.astype(F32)[None, :]
    return pl.pallas_call(
        functools.partial(_proj_kernel, splits=splits),
        out_shape=(jax.ShapeDtypeStruct((L, hgw), F32), jax.ShapeDtypeStruct((L, s5w), F32),
                   jax.ShapeDtypeStruct((L // tm, qkw, tm), BF16), jax.ShapeDtypeStruct((L, qkw), BF16),
                   jax.ShapeDtypeStruct((L // tm, vw, tm), BF16)),
        grid=(L // tm,),
        in_specs=[row(D), full(gain2), full(w_bf16), row(LANES), row(LANES), full(qg), full(kg), full(ones)],
        out_specs=(row(hgw), row(s5w), pl.BlockSpec((1, qkw, tm), lambda i: (i, 0, 0)), row(qkw),
                   pl.BlockSpec((1, vw, tm), lambda i: (i, 0, 0))),
        compiler_params=_cparams(("parallel",)),
        name="in_proj",
    )(x2, gain2, w_bf16, cos, sin, qg, kg, ones)


def _hgrn_kernel(hg_ref, lbraw_ref, gn_ref, mstack_ref, pmask_ref, ones_ref, o_ref, st_ref,
                 *, layer, nchunk, width, heads):
    @pl.when(pl.program_id(0) == 0)
    def _():
        st_ref[...] = jnp.zeros_like(st_ref)

    lbr = lbraw_ref[...]
    e = jnp.exp(lbr - jnp.max(lbr, axis=0, keepdims=True))
    sm = e / jnp.sum(e, axis=0, keepdims=True)
    lb = jnp.zeros((1, width), F32)
    for r in range(1, layer + 1):
        lb = lb + sm[r:r + 1, :]
    lbf = jnp.clip(lb, 0.0, 1.0)
    log_lb = jnp.log(lbf)
    log_1m_lb = jnp.log1p(-lbf)

    row = lax.broadcasted_iota(jnp.int32, (HG_CHUNK, 1), 0)
    gn = gn_ref[...]
    ones = ones_ref[...]

    def chunk(c, carry):
        rows = pl.ds(pl.multiple_of(c * HG_CHUNK, HG_CHUNK), HG_CHUNK)
        q = hg_ref[rows, 0:width]
        fr = hg_ref[rows, width:2 * width]
        vi = hg_ref[rows, 2 * width:3 * width]
        g = hg_ref[rows, 3 * width:4 * width]
        qh = q * jax.nn.sigmoid(q)
        log_sig = jnp.minimum(fr, 0.0) - jnp.log1p(jnp.exp(-jnp.abs(fr)))
        t2 = log_1m_lb + log_sig
        lf = jnp.maximum(log_lb, t2) + jnp.log1p(jnp.exp(-jnp.abs(log_lb - t2)))
        kh = (1.0 - lbf) * jax.nn.sigmoid(-fr)

        hi, lo = _split_bf16(lf)
        r = _dot(mstack_ref[...], jnp.concatenate([hi, lo], axis=1))
        r = r[:, 0:width] + r[:, width:2 * width]
        b = r[0:HG_CHUNK]
        b_last = b[HG_CHUNK - 1:HG_CHUNK, :]

        qls, kls = [], []
        for lv in range(HG_LEVELS):
            hblk = 1 << lv
            w = jnp.exp(-jnp.abs(b - r[(lv + 1) * HG_CHUNK:(lv + 2) * HG_CHUNK]))
            odd = ((row // hblk) % 2) == 1
            qls.append(jnp.where(odd, qh * w, 0.0).astype(BF16))
            kls.append(jnp.where(odd, 0.0, kh * w).astype(BF16))

        o_diag = _group_sum(qh * kh, ones) * vi
        qe = (qh * jnp.exp(b)).astype(BF16)
        kd = (kh * jnp.exp(b_last - b)).astype(BF16)
        dec = jnp.exp(b_last)
        vb = vi.astype(BF16)

        outs = []
        for hd in range(heads):
            hs = slice(hd * HEAD_DIM, (hd + 1) * HEAD_DIM)
            sc = jnp.zeros((HG_CHUNK, HG_CHUNK), F32)
            for lv in range(HG_LEVELS):
                sc = sc + pmask_ref[lv] * _dot_nt(qls[lv][:, hs], kls[lv][:, hs])
            st = st_ref[hd]
            v_h = vb[:, hs]
            o_h = _dot(sc.astype(BF16), v_h) + _dot_nt(qe[:, hs], st.astype(BF16)) + o_diag[:, hs]
            st_ref[hd] = st * dec[:, hs] + _dot_tn(v_h, kd[:, hs])
            ms = jnp.mean(o_h * o_h, axis=-1, keepdims=True)
            outs.append(o_h * lax.rsqrt(ms + EPS) * gn)
        o = jnp.concatenate(outs, axis=1) * (g * jax.nn.sigmoid(g))
        o_ref[rows, :] = o.astype(BF16)
        return carry

    lax.fori_loop(0, nchunk, chunk, 0)


def _hgrn_constants():
    t = np.arange(HG_CHUNK)
    mats = [(t[None, :] <= t[:, None]).astype(np.float32)]
    pmasks = []
    for lv in range(HG_LEVELS):
        hblk = 1 << lv
        boundary = (t // (2 * hblk)) * 2 * hblk + hblk - 1
        mats.append((t[None, :] <= boundary[:, None]).astype(np.float32))
        pmasks.append((t[None, :] // (2 * hblk) == t[:, None] // (2 * hblk)).astype(np.float32))
    return np.concatenate(mats, axis=0), np.stack(pmasks, axis=0)


def _hgrn(hg, lb_raw, gn_gain, *, layer, tt):
    L, w4 = hg.shape
    width = w4 // 4
    heads = width // HEAD_DIM
    mstack_np, pmask_np = _hgrn_constants()
    mstack = jnp.asarray(mstack_np).astype(BF16)
    pmask = jnp.asarray(pmask_np)
    ones = jnp.asarray(np.kron(np.eye(heads), np.ones((HEAD_DIM, HEAD_DIM))).astype(np.float32)).astype(BF16)
    lbr = lb_raw.astype(F32)
    gn = gn_gain.astype(F32)[None, :]
    full = lambda a: pl.BlockSpec(a.shape, lambda i: (0,) * a.ndim)
    return pl.pallas_call(
        functools.partial(_hgrn_kernel, layer=layer, nchunk=tt // HG_CHUNK, width=width, heads=heads),
        out_shape=jax.ShapeDtypeStruct((L, width), BF16),
        grid=(L // tt,),
        in_specs=[pl.BlockSpec((tt, w4), lambda i: (i, 0)), full(lbr), full(gn), full(mstack), full(pmask), full(ones)],
        out_specs=pl.BlockSpec((tt, width), lambda i: (i, 0)),
        scratch_shapes=[pltpu.VMEM((heads, HEAD_DIM, HEAD_DIM), F32)],
        compiler_params=_cparams(("arbitrary",)),
        name="hgrn2",
    )(hg, lbr, gn, mstack, pmask, ones)


def _s5_kernel(u_ref, bblk_ref, cblk_ref, d_ref, tr_ref, ti_ref, pr_ref, pi_ref, wglu_ref, gain_ref,
               o_ref, bu_ref, cr_ref, ci_ref, *, nblk, nstate):
    @pl.when(pl.program_id(0) == 0)
    def _():
        cr_ref[...] = jnp.zeros_like(cr_ref)
        ci_ref[...] = jnp.zeros_like(ci_ref)

    u = u_ref[...]
    bu_ref[...] = _dot(u.astype(BF16), bblk_ref[...])
    pr = pr_ref[...]
    pi = pi_ref[...]

    def blk(i, carry):
        cr, ci = carry
        rows = pl.ds(pl.multiple_of(i * SUBLANES, SUBLANES), SUBLANES)
        xr = bu_ref[rows, 0:nstate]
        xi = bu_ref[rows, nstate:2 * nstate]
        for lv in range(3):
            sh = 1 << lv
            sr = pltpu.roll(xr, sh, 0)
            si = pltpu.roll(xi, sh, 0)
            tr = tr_ref[lv]
            ti = ti_ref[lv]
            xr, xi = xr + (tr * sr - ti * si), xi + (tr * si + ti * sr)
        xr, xi = xr + (pr * cr - pi * ci), xi + (pr * ci + pi * cr)
        bu_ref[rows, 0:nstate] = xr
        bu_ref[rows, nstate:2 * nstate] = xi
        return xr[SUBLANES - 1:SUBLANES, :], xi[SUBLANES - 1:SUBLANES, :]

    cr, ci = lax.fori_loop(0, nblk, blk, (cr_ref[...], ci_ref[...]))
    cr_ref[...] = cr
    ci_ref[...] = ci

    y = _dot(bu_ref[...].astype(BF16), cblk_ref[...]) + d_ref[...] * u
    y = jax.nn.gelu(y)
    z = y * jax.nn.sigmoid(_dot(y.astype(BF16), wglu_ref[...]))
    var = jnp.mean(z * z, axis=-1, keepdims=True)
    o_ref[...] = (z * lax.rsqrt(var + EPS) * gain_ref[...]).astype(BF16)


def _s5(u, a_re, a_im, log_dt, b_re, b_im, c_re, c_im, d_skip, w_glu, out_gain, *, tt):
    L, width = u.shape
    G, P = a_re.shape
    nstate = G * P
    lam_re = jnp.minimum(a_re.astype(F32), -S5_MIN_NEG)
    lam_im = a_im.astype(F32)
    dt = jnp.exp(log_dt.astype(F32))
    mag = jnp.exp(lam_re * dt)
    abar_re, abar_im = mag * jnp.cos(lam_im * dt), mag * jnp.sin(lam_im * dt)
    den = lam_re * lam_re + lam_im * lam_im
    sc_re = ((abar_re - 1.0) * lam_re + abar_im * lam_im) / den
    sc_im = (abar_im * lam_re - (abar_re - 1.0) * lam_im) / den
    bre, bim = b_re.astype(F32), b_im.astype(F32)
    bbar_re = bre * sc_re[..., None] - bim * sc_im[..., None]
    bbar_im = bre * sc_im[..., None] + bim * sc_re[..., None]
    eye = jnp.eye(G, dtype=F32)
    bd = lambda m: jnp.einsum('gph,gk->ghkp', m, eye).reshape(G * S5_GROUP, nstate)
    bblk = jnp.concatenate([bd(bbar_re), bd(bbar_im)], axis=1).astype(BF16)
    cd = lambda m: jnp.einsum('ghp,gk->gpkh', m, eye).reshape(nstate, G * S5_GROUP)
    cblk = jnp.concatenate([cd(c_re.astype(F32)), cd(-c_im.astype(F32))], axis=0).astype(BF16)
    kpow = jnp.arange(1, SUBLANES + 1, dtype=F32)[:, None]
    pmag = jnp.exp((lam_re * dt).reshape(1, nstate) * kpow)
    parg = (lam_im * dt).reshape(1, nstate) * kpow
    p_re, p_im = pmag * jnp.cos(parg), pmag * jnp.sin(parg)
    rowi = np.arange(SUBLANES)[:, None]
    tmask = jnp.asarray(np.stack([(rowi >= (1 << lv)).astype(np.float32) for lv in range(3)]))
    t_re = tmask * jnp.stack([p_re[(1 << lv) - 1] for lv in range(3)])[:, None, :]
    t_im = tmask * jnp.stack([p_im[(1 << lv) - 1] for lv in range(3)])[:, None, :]
    d2 = d_skip.astype(F32).reshape(1, width)
    gain2 = out_gain.astype(F32)[None, :]
    wglu = w_glu.astype(BF16)
    full = lambda a: pl.BlockSpec(a.shape, lambda i: (0,) * a.ndim)
    return pl.pallas_call(
        functools.partial(_s5_kernel, nblk=tt // SUBLANES, nstate=nstate),
        out_shape=jax.ShapeDtypeStruct((L, width), BF16),
        grid=(L // tt,),
        in_specs=[pl.BlockSpec((tt, width), lambda i: (i, 0)), full(bblk), full(cblk), full(d2),
                  full(t_re), full(t_im), full(p_re), full(p_im), full(wglu), full(gain2)],
        out_specs=pl.BlockSpec((tt, width), lambda i: (i, 0)),
        scratch_shapes=[pltpu.VMEM((tt, 2 * nstate), F32), pltpu.VMEM((1, nstate), F32), pltpu.VMEM((1, nstate), F32)],
        compiler_params=_cparams(("arbitrary",)),
        name="s5",
    )(u, bblk, cblk, d2, t_re, t_im, p_re, p_im, wglu, gain2)


def _attn_kernel(posq_ref, posk_ref, q_ref, k_ref, vt_ref, lq1_ref, lk1_ref, lq2_ref, lk2_ref, sg_ref,
                 o_ref, m_ref, l_ref, acc_ref, sa_ref, sb_ref, mba_ref, mbb_ref, p_ref, *, tq, lambda_init):
    qi = pl.program_id(1)
    qt = q_ref[0]
    feat = lax.broadcasted_iota(jnp.int32, qt.shape, 0)
    zero = jnp.zeros_like(qt)
    q2t = jnp.concatenate([jnp.where(feat < HEAD_DIM, qt, zero), jnp.where(feat < HEAD_DIM, zero, qt)], axis=1)

    def produce(j, s_dst, mb_dst):
        kb = k_ref[pl.ds(pl.multiple_of(j * tq, tq), tq), :]
        sc = _dot(kb, q2t)
        s_dst[...] = sc
        mb_dst[...] = jnp.max(sc, axis=0, keepdims=True)

    def step(j, src, dst, mask):
        s_src, mb_src = src
        pv = _dot(vt_ref[jnp.maximum(j - 1, 0)], p_ref[...])
        if dst is not None:
            produce(j + 1, *dst)
        s = s_src[...]
        if mask is None:
            m_blk = mb_src[...]
        else:
            s = jnp.where(jnp.concatenate([mask, mask], axis=1), s, -jnp.inf)
            m_blk = jnp.max(s, axis=0, keepdims=True)
        m_old = m_ref[...]
        m_new = jnp.maximum(m_old, m_blk)
        alpha = jnp.exp2(m_old - m_new)
        p = jnp.exp2(s - m_new)
        l_ref[...] = alpha * l_ref[...] + jnp.sum(p, axis=0, keepdims=True)
        m_ref[...] = m_new
        acc_ref[...] = alpha * (acc_ref[...] + pv)
        p_ref[...] = p.astype(BF16)

    buf_a = (sa_ref, mba_ref)
    buf_b = (sb_ref, mbb_ref)
    m_ref[...] = jnp.full(m_ref.shape, -jnp.inf, F32)
    l_ref[...] = jnp.zeros_like(l_ref)
    acc_ref[...] = jnp.zeros_like(acc_ref)
    p_ref[...] = jnp.zeros_like(p_ref)
    produce(0, *buf_a)

    def body(t, carry):
        step(2 * t, buf_a, buf_b, None)
        step(2 * t + 1, buf_b, buf_a, None)
        return carry

    lax.fori_loop(0, qi // 2, body, 0)

    mask = posk_ref[...] <= posq_ref[...]

    @pl.when(qi % 2 == 0)
    def _():
        step(qi, buf_a, None, mask)

    @pl.when(qi % 2 == 1)
    def _():
        step(qi - 1, buf_a, buf_b, None)
        step(qi, buf_b, None, mask)

    acc_ref[...] = acc_ref[...] + _dot(vt_ref[qi], p_ref[...])

    lam = (jnp.exp(jnp.sum(lq1_ref[...] * lk1_ref[...], axis=-1, keepdims=True))
           - jnp.exp(jnp.sum(lq2_ref[...] * lk2_ref[...], axis=-1, keepdims=True)) + lambda_init)
    ot = acc_ref[...] / l_ref[...]
    ot = ot[:, 0:tq] - lam * ot[:, tq:2 * tq]
    ms = jnp.mean(ot * ot, axis=0, keepdims=True)
    ot = ot * lax.rsqrt(ms + EPS) * (1.0 - lambda_init)
    o_ref[...] = (jnp.transpose(ot) * sg_ref[...]).astype(BF16)


def _attn(q, k, vt, pos_col, pos_row, lq1, lk1, lq2, lk2, subln_gain, *, lambda_init, tq):
    L, qkw = k.shape
    heads = qkw // (2 * HEAD_DIM)
    assert q.shape == (L // tq, qkw, tq)
    nblk = vt.shape[0]
    assert vt.shape[2] == tq and nblk * tq == L
    vec = lambda a: a.astype(F32)[None, :]
    lq1, lk1, lq2, lk2, sg = vec(lq1), vec(lk1), vec(lq2), vec(lk2), vec(subln_gain)
    full = lambda a: pl.BlockSpec(a.shape, lambda h, i: (0,) * a.ndim)
    return pl.pallas_call(
        functools.partial(_attn_kernel, tq=tq, lambda_init=lambda_init),
        out_shape=jax.ShapeDtypeStruct((L, qkw), BF16),
        grid=(heads, L // tq),
        in_specs=[pl.BlockSpec((1, tq), lambda h, i: (0, i)),
                  pl.BlockSpec((tq, 1), lambda h, i: (i, 0)),
                  pl.BlockSpec((1, LANES, tq), lambda h, i: (i, h, 0)),
                  pl.BlockSpec((L, LANES), lambda h, i: (0, h)),
                  pl.BlockSpec((nblk, LANES, tq), lambda h, i: (0, h, 0)),
                  full(lq1), full(lk1), full(lq2), full(lk2), full(sg)],
        out_specs=pl.BlockSpec((tq, LANES), lambda h, i: (i, h)),
        scratch_shapes=[pltpu.VMEM((1, 2 * tq), F32), pltpu.VMEM((1, 2 * tq), F32), pltpu.VMEM((LANES, 2 * tq), F32),
                        pltpu.VMEM((tq, 2 * tq), F32), pltpu.VMEM((tq, 2 * tq), F32),
                        pltpu.VMEM((1, 2 * tq), F32), pltpu.VMEM((1, 2 * tq), F32), pltpu.VMEM((tq, 2 * tq), BF16)],
        compiler_params=_cparams(("arbitrary", "arbitrary")),
        name="diff_attn",
    )(pos_row, pos_col, q, k, vt, lq1, lk1, lq2, lk2, sg)


def _mlp_kernel(x_ref, oa_ref, ob_ref, oc_ref, wout_ref, g_ref, wup_ref, wdn_ref, o_ref,
                xres_ref, h_ref, acc_ref, *, wa, wb):
    k = pl.program_id(1)

    @pl.when(k == 0)
    def _():
        xn = (x_ref[...] + _dot(oa_ref[...], wout_ref[0:wa, :]) + _dot(ob_ref[...], wout_ref[wa:wa + wb, :])
              + _dot(oc_ref[...], wout_ref[wa + wb:, :]))
        xres_ref[...] = xn
        var = jnp.mean(xn * xn, axis=-1, keepdims=True)
        h_ref[...] = (xn * lax.rsqrt(var + EPS) * g_ref[...]).astype(BF16)
        acc_ref[...] = jnp.zeros_like(acc_ref)

    up = _dot(h_ref[...], wup_ref[...])
    act = jnp.square(jnp.maximum(up, 0.0))
    acc_ref[...] += _dot(act.astype(BF16), wdn_ref[...])

    @pl.when(k == pl.num_programs(1) - 1)
    def _():
        o_ref[...] = xres_ref[...] + acc_ref[...]


def _mlp(x2, oa, ob, oc, wout, gain, wup, wdn, *, tm, tf):
    L, D = x2.shape
    dff = wup.shape[1]
    wa, wb, wc = oa.shape[1], ob.shape[1], oc.shape[1]
    gain2 = gain.astype(F32)[None, :]
    row = lambda w: pl.BlockSpec((tm, w), lambda i, k: (i, 0))
    return pl.pallas_call(
        functools.partial(_mlp_kernel, wa=wa, wb=wb),
        out_shape=jax.ShapeDtypeStruct((L, D), F32),
        grid=(L // tm, dff // tf),
        in_specs=[row(D), row(wa), row(wb), row(wc),
                  pl.BlockSpec((D, D), lambda i, k: (0, 0)), pl.BlockSpec((1, D), lambda i, k: (0, 0)),
                  pl.BlockSpec((D, tf), lambda i, k: (0, k)), pl.BlockSpec((tf, D), lambda i, k: (k, 0))],
        out_specs=row(D),
        scratch_shapes=[pltpu.VMEM((tm, D), F32), pltpu.VMEM((tm, D), BF16), pltpu.VMEM((tm, D), F32)],
        compiler_params=_cparams(("parallel", "arbitrary")),
        name="out_proj_mlp",
    )(x2, oa, ob, oc, wout, gain2, wup, wdn)


def kernel(x, positions, hgrn_lower_bounds, attn_norm_gain, w_in, hg_norm_gain, s5_a_re, s5_a_im, s5_log_dt, s5_b_re, s5_b_im, s5_c_re, s5_c_im, s5_d, s5_w_glu, s5_norm_gain, da_q_norm_gain, da_k_norm_gain, da_lambda_q1, da_lambda_k1, da_lambda_q2, da_lambda_k2, da_subln_gain, w_out, mlp_norm_gain, w_mlp_up, w_mlp_down):
    batch, L, D = x.shape
    assert batch == 1
    depth = w_in.shape[0]
    tile = min(L, 512)
    x2 = x.reshape(L, D)
    pos_col = positions.reshape(L, 1)
    pos_row = positions.reshape(1, L)
    cos, sin = _rope_tables(pos_col)
    for l in range(depth):
        lambda_init = 0.8 - 0.6 * math.exp(-0.3 * l)
        hg, su, q, k, v = _proj(x2, attn_norm_gain[l], w_in[l].astype(BF16), cos, sin,
                                da_q_norm_gain[l], da_k_norm_gain[l], tm=tile)
        o_a = _hgrn(hg, hgrn_lower_bounds, hg_norm_gain[l], layer=l, tt=tile)
        o_b = _s5(su, s5_a_re[l], s5_a_im[l], s5_log_dt[l], s5_b_re[l], s5_b_im[l], s5_c_re[l], s5_c_im[l],
                  s5_d[l], s5_w_glu[l], s5_norm_gain[l], tt=tile)
        o_c = _attn(q, k, v, pos_col, pos_row, da_lambda_q1[l], da_lambda_k1[l], da_lambda_q2[l],
                    da_lambda_k2[l], da_subln_gain[l], lambda_init=lambda_init, tq=tile)
        x2 = _mlp(x2, o_a, o_b, o_c, w_out[l].astype(BF16), mlp_norm_gain[l],
                  w_mlp_up[l].astype(BF16), w_mlp_down[l].astype(BF16), tm=tile, tf=2048)
    return x2.reshape(batch, L, D)
```

```python
import functools
import math

import numpy as np
import jax
import jax.numpy as jnp
from jax import lax
from jax.experimental import pallas as pl
from jax.experimental.pallas import tpu as pltpu

F32 = jnp.float32
BF16 = jnp.bfloat16

EPS = 1e-6
ROPE_THETA = 10000.0
S5_MIN_NEG = 1e-4
LOG2_E = math.log2(math.e)

HEAD_DIM = 64
HG_CHUNK = 128
HG_LEVELS = 7
S5_GROUP = 16
S5_STATE = 64
SUBLANES = 8
LANES = 128
MXU_TILE = 256

VMEM_LIMIT = 56 * 1024 * 1024


def _cparams(semantics):
    return pltpu.CompilerParams(dimension_semantics=semantics, vmem_limit_bytes=VMEM_LIMIT)


def _dot(a, b):
    return jnp.dot(a, b, preferred_element_type=F32)


def _dot_nt(a, b):
    return lax.dot_general(a, b, (((1,), (1,)), ((), ())), preferred_element_type=F32)


def _dot_tn(a, b):
    return lax.dot_general(a, b, (((0,), (0,)), ((), ())), preferred_element_type=F32)


def _split_bf16(t):
    hi = t.astype(BF16)
    lo = (t - hi.astype(F32)).astype(BF16)
    return hi, lo


def _group_sum(t, ones_blockdiag):
    hi, lo = _split_bf16(t)
    return _dot(hi, ones_blockdiag) + _dot(lo, ones_blockdiag)


def _rope_kernel(pos_ref, invf_ref, cos_ref, sin_ref):
    ang = pos_ref[...].astype(F32) * invf_ref[...]
    lane = lax.broadcasted_iota(jnp.int32, ang.shape, 1)
    first_half = (lane % HEAD_DIM) < (HEAD_DIM // 2)
    s = jnp.sin(ang)
    cos_ref[...] = jnp.cos(ang)
    sin_ref[...] = jnp.where(first_half, -s, s)


def _rope_tables(pos_col):
    L = pos_col.shape[0]
    tr = min(L, 2048)
    inv_freq = ROPE_THETA ** (-jnp.arange(0, HEAD_DIM, 2, dtype=F32) / HEAD_DIM)
    invf = jnp.tile(inv_freq, LANES // (HEAD_DIM // 2))[None, :]
    return pl.pallas_call(
        _rope_kernel,
        out_shape=(jax.ShapeDtypeStruct((L, LANES), F32), jax.ShapeDtypeStruct((L, LANES), F32)),
        grid=(L // tr,),
        in_specs=[pl.BlockSpec((tr, 1), lambda i: (i, 0)), pl.BlockSpec((1, LANES), lambda i: (0, 0))],
        out_specs=(pl.BlockSpec((tr, LANES), lambda i: (i, 0)), pl.BlockSpec((tr, LANES), lambda i: (i, 0))),
        compiler_params=_cparams(("parallel",)),
        name="rope_tables",
    )(pos_col, invf)


def _proj_kernel(x_ref, g_ref, w_ref, cos_ref, sin_ref, qg_ref, kg_ref, ones_ref,
                 hg_ref, su_ref, q_ref, k_ref, v_ref, *, splits):
    x = x_ref[...]
    var = jnp.mean(x * x, axis=-1, keepdims=True)
    h = (x * lax.rsqrt(var + EPS) * g_ref[...]).astype(BF16)
    c0, c1, c2, c3, c4, c5 = splits
    hg_ref[...] = _dot(h, w_ref[:, c0:c1])
    su_ref[...] = _dot(h, w_ref[:, c1:c2])
    v_ref[0] = jnp.transpose(_dot(h, w_ref[:, c4:c5])).astype(BF16)

    cos = cos_ref[...]
    sin = sin_ref[...]
    lane = lax.broadcasted_iota(jnp.int32, cos.shape, 1)
    first_half = (lane % HEAD_DIM) < (HEAD_DIM // 2)
    ones = ones_ref[...]

    def norm_rope(t, gain, scale):
        tt = t * t
        ms = jnp.concatenate([_group_sum(tt[:, c:c + MXU_TILE], ones) for c in range(0, t.shape[1], MXU_TILE)], axis=1)
        tn = t * lax.rsqrt(ms * (1.0 / HEAD_DIM) + EPS) * gain
        outs = []
        for j in range(t.shape[1] // LANES):
            tj = tn[:, j * LANES:(j + 1) * LANES]
            partner = jnp.where(first_half,
                                pltpu.roll(tj, LANES - HEAD_DIM // 2, 1),
                                pltpu.roll(tj, HEAD_DIM // 2, 1))
            outs.append((tj * cos + partner * sin) * scale)
        return jnp.concatenate(outs, axis=1)

    q = norm_rope(_dot(h, w_ref[:, c2:c3]), qg_ref[...], LOG2_E * HEAD_DIM ** -0.5)
    q_ref[0] = jnp.transpose(q).astype(BF16)
    k_ref[...] = norm_rope(_dot(h, w_ref[:, c3:c4]), kg_ref[...], 1.0).astype(BF16)


def _proj(x2, gain, w_bf16, cos, sin, q_gain, k_gain, *, tm):
    L, D = x2.shape
    ncols = w_bf16.shape[1]
    hgw, s5w, qkw, vw = 4 * (D // 4), D // 4, D // 2, D // 2
    splits = (0, hgw, hgw + s5w, hgw + s5w + qkw, hgw + s5w + 2 * qkw, hgw + s5w + 2 * qkw + vw)
    assert splits[-1] == ncols
    assert qkw % MXU_TILE == 0
    ones = jnp.asarray(np.kron(np.eye(MXU_TILE // HEAD_DIM), np.ones((HEAD_DIM, HEAD_DIM))).astype(np.float32)).astype(BF16)
    qg = jnp.tile(q_gain.astype(F32), qkw // HEAD_DIM)[None, :]
    kg = jnp.tile(k_gain.astype(F32), qkw // HEAD_DIM)[None, :]
    row = lambda w: pl.BlockSpec((tm, w), lambda i: (i, 0))
    full = lambda a: pl.BlockSpec(a.shape, lambda i: (0,) * a.ndim)
    gain2 = gain.astype(F32)[None, :]
    return pl.pallas_call(
        functools.partial(_proj_kernel, splits=splits),
        out_shape=(jax.ShapeDtypeStruct((L, hgw), F32), jax.ShapeDtypeStruct((L, s5w), F32),
                   jax.ShapeDtypeStruct((L // tm, qkw, tm), BF16), jax.ShapeDtypeStruct((L, qkw), BF16),
                   jax.ShapeDtypeStruct((L // tm, vw, tm), BF16)),
        grid=(L // tm,),
        in_specs=[row(D), full(gain2), full(w_bf16), row(LANES), row(LANES), full(qg), full(kg), full(ones)],
        out_specs=(row(hgw), row(s5w), pl.BlockSpec((1, qkw, tm), lambda i: (i, 0, 0)), row(qkw),
                   pl.BlockSpec((1, vw, tm), lambda i: (i, 0, 0))),
        compiler_params=_cparams(("parallel",)),
        name="in_proj",
    )(x2, gain2, w_bf16, cos, sin, qg, kg, ones)


def _hgrn_kernel(hg_ref, lbraw_ref, gn_ref, mstack_ref, pmask_ref, ones_ref, o_ref, st_ref,
                 *, layer, nchunk, width, heads):
    @pl.when(pl.program_id(0) == 0)
    def _():
        st_ref[...] = jnp.zeros_like(st_ref)

    lbr = lbraw_ref[...]
    e = jnp.exp(lbr - jnp.max(lbr, axis=0, keepdims=True))
    sm = e / jnp.sum(e, axis=0, keepdims=True)
    lb = jnp.zeros((1, width), F32)
    for r in range(1, layer + 1):
        lb = lb + sm[r:r + 1, :]
    lbf = jnp.clip(lb, 0.0, 1.0)
    log_lb = jnp.log(lbf)
    log_1m_lb = jnp.log1p(-lbf)

    row = lax.broadcasted_iota(jnp.int32, (HG_CHUNK, 1), 0)
    gn = gn_ref[...]
    ones = ones_ref[...]

    def chunk(c, carry):
        rows = pl.ds(pl.multiple_of(c * HG_CHUNK, HG_CHUNK), HG_CHUNK)
        q = hg_ref[rows, 0:width]
        fr = hg_ref[rows, width:2 * width]
        vi = hg_ref[rows, 2 * width:3 * width]
        g = hg_ref[rows, 3 * width:4 * width]
        qh = q * jax.nn.sigmoid(q)
        log_sig = jnp.minimum(fr, 0.0) - jnp.log1p(jnp.exp(-jnp.abs(fr)))
        t2 = log_1m_lb + log_sig
        lf = jnp.maximum(log_lb, t2) + jnp.log1p(jnp.exp(-jnp.abs(log_lb - t2)))
        kh = (1.0 - lbf) * jax.nn.sigmoid(-fr)

        hi, lo = _split_bf16(lf)
        r = _dot(mstack_ref[...], jnp.concatenate([hi, lo], axis=1))
        r = r[:, 0:width] + r[:, width:2 * width]
        b = r[0:HG_CHUNK]
        b_last = b[HG_CHUNK - 1:HG_CHUNK, :]

        qls, kls = [], []
        for lv in range(HG_LEVELS):
            hblk = 1 << lv
            w = jnp.exp(-jnp.abs(b - r[(lv + 1) * HG_CHUNK:(lv + 2) * HG_CHUNK]))
            odd = ((row // hblk) % 2) == 1
            qls.append(jnp.where(odd, qh * w, 0.0).astype(BF16))
            kls.append(jnp.where(odd, 0.0, kh * w).astype(BF16))

        o_diag = _group_sum(qh * kh, ones) * vi
        qe = (qh * jnp.exp(b)).astype(BF16)
        kd = (kh * jnp.exp(b_last - b)).astype(BF16)
        dec = jnp.exp(b_last)
        vb = vi.astype(BF16)

        outs = []
        for hd in range(heads):
            hs = slice(hd * HEAD_DIM, (hd + 1) * HEAD_DIM)
            sc = jnp.zeros((HG_CHUNK, HG_CHUNK), F32)
            for lv in range(HG_LEVELS):
                sc = sc + pmask_ref[lv] * _dot_nt(qls[lv][:, hs], kls[lv][:, hs])
            st = st_ref[hd]
            v_h = vb[:, hs]
            o_h = _dot(sc.astype(BF16), v_h) + _dot_nt(qe[:, hs], st.astype(BF16)) + o_diag[:, hs]
            st_ref[hd] = st * dec[:, hs] + _dot_tn(v_h, kd[:, hs])
            ms = jnp.mean(o_h * o_h, axis=-1, keepdims=True)
            outs.append(o_h * lax.rsqrt(ms + EPS) * gn)
        o = jnp.concatenate(outs, axis=1) * (g * jax.nn.sigmoid(g))
        o_ref[rows, :] = o.astype(BF16)
        return carry

    lax.fori_loop(0, nchunk, chunk, 0)


def _hgrn_constants():
    t = np.arange(HG_CHUNK)
    mats = [(t[None, :] <= t[:, None]).astype(np.float32)]
    pmasks = []
    for lv in range(HG_LEVELS):
        hblk = 1 << lv
        boundary = (t // (2 * hblk)) * 2 * hblk + hblk - 1
        mats.append((t[None, :] <= boundary[:, None]).astype(np.float32))
        pmasks.append((t[None, :] // (2 * hblk) == t[:, None] // (2 * hblk)).astype(np.float32))
    return np.concatenate(mats, axis=0), np.stack(pmasks, axis=0)


def _hgrn(hg, lb_raw, gn_gain, *, layer, tt):
    L, w4 = hg.shape
    width = w4 // 4
    heads = width // HEAD_DIM
    mstack_np, pmask_np = _hgrn_constants()
    mstack = jnp.asarray(mstack_np).astype(BF16)
    pmask = jnp.asarray(pmask_np)
    ones = jnp.asarray(np.kron(np.eye(heads), np.ones((HEAD_DIM, HEAD_DIM))).astype(np.float32)).astype(BF16)
    lbr = lb_raw.astype(F32)
    gn = gn_gain.astype(F32)[None, :]
    full = lambda a: pl.BlockSpec(a.shape, lambda i: (0,) * a.ndim)
    return pl.pallas_call(
        functools.partial(_hgrn_kernel, layer=layer, nchunk=tt // HG_CHUNK, width=width, heads=heads),
        out_shape=jax.ShapeDtypeStruct((L, width), BF16),
        grid=(L // tt,),
        in_specs=[pl.BlockSpec((tt, w4), lambda i: (i, 0)), full(lbr), full(gn), full(mstack), full(pmask), full(ones)],
        out_specs=pl.BlockSpec((tt, width), lambda i: (i, 0)),
        scratch_shapes=[pltpu.VMEM((heads, HEAD_DIM, HEAD_DIM), F32)],
        compiler_params=_cparams(("arbitrary",)),
        name="hgrn2",
    )(hg, lbr, gn, mstack, pmask, ones)


def _s5_kernel(u_ref, bblk_ref, cblk_ref, d_ref, tr_ref, ti_ref, pr_ref, pi_ref, wglu_ref, gain_ref,
               o_ref, bu_ref, cr_ref, ci_ref, *, nblk, nstate):
    @pl.when(pl.program_id(0) == 0)
    def _():
        cr_ref[...] = jnp.zeros_like(cr_ref)
        ci_ref[...] = jnp.zeros_like(ci_ref)

    u = u_ref[...]
    bu_ref[...] = _dot(u.astype(BF16), bblk_ref[...])
    pr = pr_ref[...]
    pi = pi_ref[...]

    def blk(i, carry):
        cr, ci = carry
        rows = pl.ds(pl.multiple_of(i * SUBLANES, SUBLANES), SUBLANES)
        xr = bu_ref[rows, 0:nstate]
        xi = bu_ref[rows, nstate:2 * nstate]
        for lv in range(3):
            sh = 1 << lv
            sr = pltpu.roll(xr, sh, 0)
            si = pltpu.roll(xi, sh, 0)
            tr = tr_ref[lv]
            ti = ti_ref[lv]
            xr, xi = xr + (tr * sr - ti * si), xi + (tr * si + ti * sr)
        xr, xi = xr + (pr * cr - pi * ci), xi + (pr * ci + pi * cr)
        bu_ref[rows, 0:nstate] = xr
        bu_ref[rows, nstate:2 * nstate] = xi
        return xr[SUBLANES - 1:SUBLANES, :], xi[SUBLANES - 1:SUBLANES, :]

    cr, ci = lax.fori_loop(0, nblk, blk, (cr_ref[...], ci_ref[...]))
    cr_ref[...] = cr
    ci_ref[...] = ci

    half = u.shape[0] // 2
    y = jnp.concatenate([_dot(bu_ref[r:r + half, :].astype(BF16), cblk_ref[...]) for r in (0, half)], axis=0)
    y = y + d_ref[...] * u
    y = jax.nn.gelu(y)
    z = y * jax.nn.sigmoid(_dot(y.astype(BF16), wglu_ref[...]))
    var = jnp.mean(z * z, axis=-1, keepdims=True)
    o_ref[...] = (z * lax.rsqrt(var + EPS) * gain_ref[...]).astype(BF16)


def _s5(u, a_re, a_im, log_dt, b_re, b_im, c_re, c_im, d_skip, w_glu, out_gain, *, tt):
    L, width = u.shape
    G, P = a_re.shape
    nstate = G * P
    lam_re = jnp.minimum(a_re.astype(F32), -S5_MIN_NEG)
    lam_im = a_im.astype(F32)
    dt = jnp.exp(log_dt.astype(F32))
    mag = jnp.exp(lam_re * dt)
    abar_re, abar_im = mag * jnp.cos(lam_im * dt), mag * jnp.sin(lam_im * dt)
    den = lam_re * lam_re + lam_im * lam_im
    sc_re = ((abar_re - 1.0) * lam_re + abar_im * lam_im) / den
    sc_im = (abar_im * lam_re - (abar_re - 1.0) * lam_im) / den
    bre, bim = b_re.astype(F32), b_im.astype(F32)
    bbar_re = bre * sc_re[..., None] - bim * sc_im[..., None]
    bbar_im = bre * sc_im[..., None] + bim * sc_re[..., None]
    eye = jnp.eye(G, dtype=F32)
    bd = lambda m: jnp.einsum('gph,gk->ghkp', m, eye).reshape(G * S5_GROUP, nstate)
    bblk = jnp.concatenate([bd(bbar_re), bd(bbar_im)], axis=1).astype(BF16)
    cd = lambda m: jnp.einsum('ghp,gk->gpkh', m, eye).reshape(nstate, G * S5_GROUP)
    cblk = jnp.concatenate([cd(c_re.astype(F32)), cd(-c_im.astype(F32))], axis=0).astype(BF16)
    kpow = jnp.arange(1, SUBLANES + 1, dtype=F32)[:, None]
    pmag = jnp.exp((lam_re * dt).reshape(1, nstate) * kpow)
    parg = (lam_im * dt).reshape(1, nstate) * kpow
    p_re, p_im = pmag * jnp.cos(parg), pmag * jnp.sin(parg)
    rowi = np.arange(SUBLANES)[:, None]
    tmask = jnp.asarray(np.stack([(rowi >= (1 << lv)).astype(np.float32) for lv in range(3)]))
    t_re = tmask * jnp.stack([p_re[(1 << lv) - 1] for lv in range(3)])[:, None, :]
    t_im = tmask * jnp.stack([p_im[(1 << lv) - 1] for lv in range(3)])[:, None, :]
    d2 = d_skip.astype(F32).reshape(1, width)
    gain2 = out_gain.astype(F32)[None, :]
    wglu = w_glu.astype(BF16)
    full = lambda a: pl.BlockSpec(a.shape, lambda i: (0,) * a.ndim)
    return pl.pallas_call(
        functools.partial(_s5_kernel, nblk=tt // SUBLANES, nstate=nstate),
        out_shape=jax.ShapeDtypeStruct((L, width), BF16),
        grid=(L // tt,),
        in_specs=[pl.BlockSpec((tt, width), lambda i: (i, 0)), full(bblk), full(cblk), full(d2),
                  full(t_re), full(t_im), full(p_re), full(p_im), full(wglu), full(gain2)],
        out_specs=pl.BlockSpec((tt, width), lambda i: (i, 0)),
        scratch_shapes=[pltpu.VMEM((tt, 2 * nstate), F32), pltpu.VMEM((1, nstate), F32), pltpu.VMEM((1, nstate), F32)],
        compiler_params=_cparams(("arbitrary",)),
        name="s5",
    )(u, bblk, cblk, d2, t_re, t_im, p_re, p_im, wglu, gain2)


def _attn_kernel(posq_ref, posk_ref, q_ref, k_ref, vt_ref, lq1_ref, lk1_ref, lq2_ref, lk2_ref, sg_ref,
                 o_ref, m_ref, l_ref, acc_ref, sa_ref, sb_ref, mba_ref, mbb_ref, p_ref, *, tq, lambda_init):
    qi = pl.program_id(1)
    qt = q_ref[0]
    feat = lax.broadcasted_iota(jnp.int32, qt.shape, 0)
    zero = jnp.zeros_like(qt)
    q2t = jnp.concatenate([jnp.where(feat < HEAD_DIM, qt, zero), jnp.where(feat < HEAD_DIM, zero, qt)], axis=1)

    def produce(j, s_dst, mb_dst):
        kb = k_ref[pl.ds(pl.multiple_of(j * tq, tq), tq), :]
        sc = _dot(kb, q2t)
        s_dst[...] = sc
        mb_dst[...] = jnp.max(sc, axis=0, keepdims=True)

    def step(j, src, dst, mask):
        s_src, mb_src = src
        pv = _dot(vt_ref[jnp.maximum(j - 1, 0)], p_ref[...])
        if dst is not None:
            produce(j + 1, *dst)
        s = s_src[...]
        if mask is None:
            m_blk = mb_src[...]
        else:
            s = jnp.where(jnp.concatenate([mask, mask], axis=1), s, -jnp.inf)
            m_blk = jnp.max(s, axis=0, keepdims=True)
        m_old = m_ref[...]
        m_new = jnp.maximum(m_old, m_blk)
        alpha = jnp.exp2(m_old - m_new)
        p = jnp.exp2(s - m_new)
        l_ref[...] = alpha * l_ref[...] + jnp.sum(p, axis=0, keepdims=True)
        m_ref[...] = m_new
        acc_ref[...] = alpha * (acc_ref[...] + pv)
        p_ref[...] = p.astype(BF16)

    buf_a = (sa_ref, mba_ref)
    buf_b = (sb_ref, mbb_ref)
    m_ref[...] = jnp.full(m_ref.shape, -jnp.inf, F32)
    l_ref[...] = jnp.zeros_like(l_ref)
    acc_ref[...] = jnp.zeros_like(acc_ref)
    p_ref[...] = jnp.zeros_like(p_ref)
    produce(0, *buf_a)

    def body(t, carry):
        step(2 * t, buf_a, buf_b, None)
        step(2 * t + 1, buf_b, buf_a, None)
        return carry

    lax.fori_loop(0, qi // 2, body, 0)

    mask = posk_ref[...] <= posq_ref[...]

    @pl.when(qi % 2 == 0)
    def _():
        step(qi, buf_a, None, mask)

    @pl.when(qi % 2 == 1)
    def _():
        step(qi - 1, buf_a, buf_b, None)
        step(qi, buf_b, None, mask)

    acc_ref[...] = acc_ref[...] + _dot(vt_ref[qi], p_ref[...])

    lam = (jnp.exp(jnp.sum(lq1_ref[...] * lk1_ref[...], axis=-1, keepdims=True))
           - jnp.exp(jnp.sum(lq2_ref[...] * lk2_ref[...], axis=-1, keepdims=True)) + lambda_init)
    ot = acc_ref[...] / l_ref[...]
    ot = ot[:, 0:tq] - lam * ot[:, tq:2 * tq]
    ms = jnp.mean(ot * ot, axis=0, keepdims=True)
    ot = ot * lax.rsqrt(ms + EPS) * (1.0 - lambda_init)
    o_ref[...] = (jnp.transpose(ot) * sg_ref[...]).astype(BF16)


def _attn(q, k, vt, pos_col, pos_row, lq1, lk1, lq2, lk2, subln_gain, *, lambda_init, tq):
    L, qkw = k.shape
    heads = qkw // (2 * HEAD_DIM)
    assert q.shape == (L // tq, qkw, tq)
    nblk = vt.shape[0]
    assert vt.shape[2] == tq and nblk * tq == L
    vec = lambda a: a.astype(F32)[None, :]
    lq1, lk1, lq2, lk2, sg = vec(lq1), vec(lk1), vec(lq2), vec(lk2), vec(subln_gain)
    full = lambda a: pl.BlockSpec(a.shape, lambda h, i: (0,) * a.ndim)
    return pl.pallas_call(
        functools.partial(_attn_kernel, tq=tq, lambda_init=lambda_init),
        out_shape=jax.ShapeDtypeStruct((L, qkw), BF16),
        grid=(heads, L // tq),
        in_specs=[pl.BlockSpec((1, tq), lambda h, i: (0, i)),
                  pl.BlockSpec((tq, 1), lambda h, i: (i, 0)),
                  pl.BlockSpec((1, LANES, tq), lambda h, i: (i, h, 0)),
                  pl.BlockSpec((L, LANES), lambda h, i: (0, h)),
                  pl.BlockSpec((nblk, LANES, tq), lambda h, i: (0, h, 0)),
                  full(lq1), full(lk1), full(lq2), full(lk2), full(sg)],
        out_specs=pl.BlockSpec((tq, LANES), lambda h, i: (i, h)),
        scratch_shapes=[pltpu.VMEM((1, 2 * tq), F32), pltpu.VMEM((1, 2 * tq), F32), pltpu.VMEM((LANES, 2 * tq), F32),
                        pltpu.VMEM((tq, 2 * tq), F32), pltpu.VMEM((tq, 2 * tq), F32),
                        pltpu.VMEM((1, 2 * tq), F32), pltpu.VMEM((1, 2 * tq), F32), pltpu.VMEM((tq, 2 * tq), BF16)],
        compiler_params=_cparams(("arbitrary", "arbitrary")),
        name="diff_attn",
    )(pos_row, pos_col, q, k, vt, lq1, lk1, lq2, lk2, sg)


def _mlp_kernel(x_ref, oa_ref, ob_ref, oc_ref, wout_ref, g_ref, wup_ref, wdn_ref, o_ref,
                xres_ref, h_ref, acc_ref, *, wa, wb):
    k = pl.program_id(1)

    @pl.when(k == 0)
    def _():
        xn = (x_ref[...] + _dot(oa_ref[...], wout_ref[0:wa, :]) + _dot(ob_ref[...], wout_ref[wa:wa + wb, :])
              + _dot(oc_ref[...], wout_ref[wa + wb:, :]))
        xres_ref[...] = xn
        var = jnp.mean(xn * xn, axis=-1, keepdims=True)
        h_ref[...] = (xn * lax.rsqrt(var + EPS) * g_ref[...]).astype(BF16)
        acc_ref[...] = jnp.zeros_like(acc_ref)

    up = _dot(h_ref[...], wup_ref[...])
    act = jnp.square(jnp.maximum(up, 0.0))
    acc_ref[...] += _dot(act.astype(BF16), wdn_ref[...])

    @pl.when(k == pl.num_programs(1) - 1)
    def _():
        o_ref[...] = xres_ref[...] + acc_ref[...]


def _mlp(x2, oa, ob, oc, wout, gain, wup, wdn, *, tm, tf):
    L, D = x2.shape
    dff = wup.shape[1]
    wa, wb, wc = oa.shape[1], ob.shape[1], oc.shape[1]
    gain2 = gain.astype(F32)[None, :]
    row = lambda w: pl.BlockSpec((tm, w), lambda i, k: (i, 0))
    return pl.pallas_call(
        functools.partial(_mlp_kernel, wa=wa, wb=wb),
        out_shape=jax.ShapeDtypeStruct((L, D), F32),
        grid=(L // tm, dff // tf),
        in_specs=[row(D), row(wa), row(wb), row(wc),
                  pl.BlockSpec((D, D), lambda i, k: (0, 0)), pl.BlockSpec((1, D), lambda i, k: (0, 0)),
                  pl.BlockSpec((D, tf), lambda i, k: (0, k)), pl.BlockSpec((tf, D), lambda i, k: (k, 0))],
        out_specs=row(D),
        scratch_shapes=[pltpu.VMEM((tm, D), F32), pltpu.VMEM((tm, D), BF16), pltpu.VMEM((tm, D), F32)],
        compiler_params=_cparams(("parallel", "arbitrary")),
        name="out_proj_mlp",
    )(x2, oa, ob, oc, wout, gain2, wup, wdn)


def kernel(x, positions, hgrn_lower_bounds, attn_norm_gain, w_in, hg_norm_gain, s5_a_re, s5_a_im, s5_log_dt, s5_b_re, s5_b_im, s5_c_re, s5_c_im, s5_d, s5_w_glu, s5_norm_gain, da_q_norm_gain, da_k_norm_gain, da_lambda_q1, da_lambda_k1, da_lambda_q2, da_lambda_k2, da_subln_gain, w_out, mlp_norm_gain, w_mlp_up, w_mlp_down):
    batch, L, D = x.shape
    assert batch == 1
    depth = w_in.shape[0]
    tile = min(L, 512)
    x2 = x.reshape(L, D)
    pos_col = positions.reshape(L, 1)
    pos_row = positions.reshape(1, L)
    cos, sin = _rope_tables(pos_col)
    for l in range(depth):
        lambda_init = 0.8 - 0.6 * math.exp(-0.3 * l)
        hg, su, q, k, v = _proj(x2, attn_norm_gain[l], w_in[l].astype(BF16), cos, sin,
                                da_q_norm_gain[l], da_k_norm_gain[l], tm=tile)
        o_a = _hgrn(hg, hgrn_lower_bounds, hg_norm_gain[l], layer=l, tt=tile)
        o_b = _s5(su, s5_a_re[l], s5_a_im[l], s5_log_dt[l], s5_b_re[l], s5_b_im[l], s5_c_re[l], s5_c_im[l],
                  s5_d[l], s5_w_glu[l], s5_norm_gain[l], tt=tile)
        o_c = _attn(q, k, v, pos_col, pos_row, da_lambda_q1[l], da_lambda_k1[l], da_lambda_q2[l],
                    da_lambda_k2[l], da_subln_gain[l], lambda_init=lambda_init, tq=tile)
        x2 = _mlp(x2, o_a, o_b, o_c, w_out[l].astype(BF16), mlp_norm_gain[l],
                  w_mlp_up[l].astype(BF16), w_mlp_down[l].astype(BF16), tm=tile, tf=2048)
    return x2.reshape(batch, L, D)
```

```python
import functools
import math

import numpy as np
import jax
import jax.numpy as jnp
from jax import lax
from jax.experimental import pallas as pl
from jax.experimental.pallas import tpu as pltpu

F32 = jnp.float32
BF16 = jnp.bfloat16

EPS = 1e-6
ROPE_THETA = 10000.0
S5_MIN_NEG = 1e-4
LOG2_E = math.log2(math.e)

HEAD_DIM = 64
HG_CHUNK = 128
HG_LEVELS = 7
S5_GROUP = 16
S5_STATE = 64
SUBLANES = 8
LANES = 128
MXU_TILE = 256

VMEM_LIMIT = 56 * 1024 * 1024


def _cparams(semantics):
    return pltpu.CompilerParams(dimension_semantics=semantics, vmem_limit_bytes=VMEM_LIMIT)


def _dot(a, b):
    return jnp.dot(a, b, preferred_element_type=F32)


def _dot_nt(a, b):
    return lax.dot_general(a, b, (((1,), (1,)), ((), ())), preferred_element_type=F32)


def _dot_tn(a, b):
    return lax.dot_general(a, b, (((0,), (0,)), ((), ())), preferred_element_type=F32)


def _split_bf16(t):
    hi = t.astype(BF16)
    lo = (t - hi.astype(F32)).astype(BF16)
    return hi, lo


def _group_sum(t, ones_blockdiag):
    hi, lo = _split_bf16(t)
    return _dot(hi, ones_blockdiag) + _dot(lo, ones_blockdiag)


def _rope_kernel(pos_ref, invf_ref, cos_ref, sin_ref):
    ang = pos_ref[...].astype(F32) * invf_ref[...]
    lane = lax.broadcasted_iota(jnp.int32, ang.shape, 1)
    first_half = (lane % HEAD_DIM) < (HEAD_DIM // 2)
    s = jnp.sin(ang)
    cos_ref[...] = jnp.cos(ang)
    sin_ref[...] = jnp.where(first_half, -s, s)


def _rope_tables(pos_col):
    L = pos_col.shape[0]
    tr = min(L, 2048)
    inv_freq = ROPE_THETA ** (-jnp.arange(0, HEAD_DIM, 2, dtype=F32) / HEAD_DIM)
    invf = jnp.tile(inv_freq, LANES // (HEAD_DIM // 2))[None, :]
    return pl.pallas_call(
        _rope_kernel,
        out_shape=(jax.ShapeDtypeStruct((L, LANES), F32), jax.ShapeDtypeStruct((L, LANES), F32)),
        grid=(L // tr,),
        in_specs=[pl.BlockSpec((tr, 1), lambda i: (i, 0)), pl.BlockSpec((1, LANES), lambda i: (0, 0))],
        out_specs=(pl.BlockSpec((tr, LANES), lambda i: (i, 0)), pl.BlockSpec((tr, LANES), lambda i: (i, 0))),
        compiler_params=_cparams(("parallel",)),
        name="rope_tables",
    )(pos_col, invf)


def _proj_kernel(x_ref, g_ref, w_ref, cos_ref, sin_ref, qg_ref, kg_ref, ones_ref,
                 hg_ref, su_ref, q_ref, k_ref, v_ref, *, splits):
    x = x_ref[...]
    var = jnp.mean(x * x, axis=-1, keepdims=True)
    h = (x * lax.rsqrt(var + EPS) * g_ref[...]).astype(BF16)
    c0, c1, c2, c3, c4, c5 = splits
    hg_ref[...] = _dot(h, w_ref[:, c0:c1])
    su_ref[...] = _dot(h, w_ref[:, c1:c2])
    v_ref[0] = jnp.transpose(_dot(h, w_ref[:, c4:c5])).astype(BF16)

    cos = cos_ref[...]
    sin = sin_ref[...]
    lane = lax.broadcasted_iota(jnp.int32, cos.shape, 1)
    first_half = (lane % HEAD_DIM) < (HEAD_DIM // 2)
    ones = ones_ref[...]

    def norm_rope(t, gain, scale):
        tt = t * t
        ms = jnp.concatenate([_group_sum(tt[:, c:c + MXU_TILE], ones) for c in range(0, t.shape[1], MXU_TILE)], axis=1)
        tn = t * lax.rsqrt(ms * (1.0 / HEAD_DIM) + EPS) * gain
        outs = []
        for j in range(t.shape[1] // LANES):
            tj = tn[:, j * LANES:(j + 1) * LANES]
            partner = jnp.where(first_half,
                                pltpu.roll(tj, LANES - HEAD_DIM // 2, 1),
                                pltpu.roll(tj, HEAD_DIM // 2, 1))
            outs.append((tj * cos + partner * sin) * scale)
        return jnp.concatenate(outs, axis=1)

    q = norm_rope(_dot(h, w_ref[:, c2:c3]), qg_ref[...], LOG2_E * HEAD_DIM ** -0.5)
    q_ref[0] = jnp.transpose(q).astype(BF16)
    k_ref[...] = norm_rope(_dot(h, w_ref[:, c3:c4]), kg_ref[...], 1.0).astype(BF16)


def _proj(x2, gain, w_bf16, cos, sin, q_gain, k_gain, *, tm):
    L, D = x2.shape
    ncols = w_bf16.shape[1]
    hgw, s5w, qkw, vw = 4 * (D // 4), D // 4, D // 2, D // 2
    splits = (0, hgw, hgw + s5w, hgw + s5w + qkw, hgw + s5w + 2 * qkw, hgw + s5w + 2 * qkw + vw)
    assert splits[-1] == ncols
    assert qkw % MXU_TILE == 0
    ones = jnp.asarray(np.kron(np.eye(MXU_TILE // HEAD_DIM), np.ones((HEAD_DIM, HEAD_DIM))).astype(np.float32)).astype(BF16)
    qg = jnp.tile(q_gain.astype(F32), qkw // HEAD_DIM)[None, :]
    kg = jnp.tile(k_gain.astype(F32), qkw // HEAD_DIM)[None, :]
    row = lambda w: pl.BlockSpec((tm, w), lambda i: (i, 0))
    full = lambda a: pl.BlockSpec(a.shape, lambda i: (0,) * a.ndim)
    gain2 = gain.astype(F32)[None, :]
    return pl.pallas_call(
        functools.partial(_proj_kernel, splits=splits),
        out_shape=(jax.ShapeDtypeStruct((L, hgw), F32), jax.ShapeDtypeStruct((L, s5w), F32),
                   jax.ShapeDtypeStruct((L // tm, qkw, tm), BF16), jax.ShapeDtypeStruct((L, qkw), BF16),
                   jax.ShapeDtypeStruct((L // tm, vw, tm), BF16)),
        grid=(L // tm,),
        in_specs=[row(D), full(gain2), full(w_bf16), row(LANES), row(LANES), full(qg), full(kg), full(ones)],
        out_specs=(row(hgw), row(s5w), pl.BlockSpec((1, qkw, tm), lambda i: (i, 0, 0)), row(qkw),
                   pl.BlockSpec((1, vw, tm), lambda i: (i, 0, 0))),
        compiler_params=_cparams(("parallel",)),
        name="in_proj",
    )(x2, gain2, w_bf16, cos, sin, qg, kg, ones)


def _hgrn_kernel(hg_ref, lbraw_ref, gn_ref, mstack_ref, pmask_ref, ones_ref, o_ref, st_ref,
                 *, layer, nchunk, width, heads):
    @pl.when(pl.program_id(0) == 0)
    def _():
        st_ref[...] = jnp.zeros_like(st_ref)

    lbr = lbraw_ref[...]
    e = jnp.exp(lbr - jnp.max(lbr, axis=0, keepdims=True))
    sm = e / jnp.sum(e, axis=0, keepdims=True)
    lb = jnp.zeros((1, width), F32)
    for r in range(1, layer + 1):
        lb = lb + sm[r:r + 1, :]
    lbf = jnp.clip(lb, 0.0, 1.0)
    log_lb = jnp.log(lbf)
    log_1m_lb = jnp.log1p(-lbf)

    row = lax.broadcasted_iota(jnp.int32, (HG_CHUNK, 1), 0)
    gn = gn_ref[...]
    ones = ones_ref[...]

    def chunk(c, carry):
        rows = pl.ds(pl.multiple_of(c * HG_CHUNK, HG_CHUNK), HG_CHUNK)
        q = hg_ref[rows, 0:width]
        fr = hg_ref[rows, width:2 * width]
        vi = hg_ref[rows, 2 * width:3 * width]
        g = hg_ref[rows, 3 * width:4 * width]
        qh = q * jax.nn.sigmoid(q)
        log_sig = jnp.minimum(fr, 0.0) - jnp.log1p(jnp.exp(-jnp.abs(fr)))
        t2 = log_1m_lb + log_sig
        lf = jnp.maximum(log_lb, t2) + jnp.log1p(jnp.exp(-jnp.abs(log_lb - t2)))
        kh = (1.0 - lbf) * jax.nn.sigmoid(-fr)

        hi, lo = _split_bf16(lf)
        r = _dot(mstack_ref[...], jnp.concatenate([hi, lo], axis=1))
        r = r[:, 0:width] + r[:, width:2 * width]
        b = r[0:HG_CHUNK]
        b_last = b[HG_CHUNK - 1:HG_CHUNK, :]

        qls, kls = [], []
        for lv in range(HG_LEVELS):
            hblk = 1 << lv
            w = jnp.exp(-jnp.abs(b - r[(lv + 1) * HG_CHUNK:(lv + 2) * HG_CHUNK]))
            odd = ((row // hblk) % 2) == 1
            qls.append(jnp.where(odd, qh * w, 0.0).astype(BF16))
            kls.append(jnp.where(odd, 0.0, kh * w).astype(BF16))

        o_diag = _group_sum(qh * kh, ones) * vi
        qe = (qh * jnp.exp(b)).astype(BF16)
        kd = (kh * jnp.exp(b_last - b)).astype(BF16)
        dec = jnp.exp(b_last)
        vb = vi.astype(BF16)

        outs = []
        for hd in range(heads):
            hs = slice(hd * HEAD_DIM, (hd + 1) * HEAD_DIM)
            sc = jnp.zeros((HG_CHUNK, HG_CHUNK), F32)
            for lv in range(HG_LEVELS):
                sc = sc + pmask_ref[lv] * _dot_nt(qls[lv][:, hs], kls[lv][:, hs])
            st = st_ref[hd]
            v_h = vb[:, hs]
            o_h = _dot(sc.astype(BF16), v_h) + _dot_nt(qe[:, hs], st.astype(BF16)) + o_diag[:, hs]
            st_ref[hd] = st * dec[:, hs] + _dot_tn(v_h, kd[:, hs])
            ms = jnp.mean(o_h * o_h, axis=-1, keepdims=True)
            outs.append(o_h * lax.rsqrt(ms + EPS) * gn)
        o = jnp.concatenate(outs, axis=1) * (g * jax.nn.sigmoid(g))
        o_ref[rows, :] = o.astype(BF16)
        return carry

    lax.fori_loop(0, nchunk, chunk, 0, unroll=2)


def _hgrn_constants():
    t = np.arange(HG_CHUNK)
    mats = [(t[None, :] <= t[:, None]).astype(np.float32)]
    pmasks = []
    for lv in range(HG_LEVELS):
        hblk = 1 << lv
        boundary = (t // (2 * hblk)) * 2 * hblk + hblk - 1
        mats.append((t[None, :] <= boundary[:, None]).astype(np.float32))
        pmasks.append((t[None, :] // (2 * hblk) == t[:, None] // (2 * hblk)).astype(np.float32))
    return np.concatenate(mats, axis=0), np.stack(pmasks, axis=0)


def _hgrn(hg, lb_raw, gn_gain, *, layer, tt):
    L, w4 = hg.shape
    width = w4 // 4
    heads = width // HEAD_DIM
    mstack_np, pmask_np = _hgrn_constants()
    mstack = jnp.asarray(mstack_np).astype(BF16)
    pmask = jnp.asarray(pmask_np)
    ones = jnp.asarray(np.kron(np.eye(heads), np.ones((HEAD_DIM, HEAD_DIM))).astype(np.float32)).astype(BF16)
    lbr = lb_raw.astype(F32)
    gn = gn_gain.astype(F32)[None, :]
    full = lambda a: pl.BlockSpec(a.shape, lambda i: (0,) * a.ndim)
    return pl.pallas_call(
        functools.partial(_hgrn_kernel, layer=layer, nchunk=tt // HG_CHUNK, width=width, heads=heads),
        out_shape=jax.ShapeDtypeStruct((L, width), BF16),
        grid=(L // tt,),
        in_specs=[pl.BlockSpec((tt, w4), lambda i: (i, 0)), full(lbr), full(gn), full(mstack), full(pmask), full(ones)],
        out_specs=pl.BlockSpec((tt, width), lambda i: (i, 0)),
        scratch_shapes=[pltpu.VMEM((heads, HEAD_DIM, HEAD_DIM), F32)],
        compiler_params=_cparams(("arbitrary",)),
        name="hgrn2",
    )(hg, lbr, gn, mstack, pmask, ones)


def _s5_kernel(u_ref, bblk_ref, cblk_ref, d_ref, tr_ref, ti_ref, pr_ref, pi_ref, wglu_ref, gain_ref,
               o_ref, bu_ref, cr_ref, ci_ref, *, nblk, nstate):
    @pl.when(pl.program_id(0) == 0)
    def _():
        cr_ref[...] = jnp.zeros_like(cr_ref)
        ci_ref[...] = jnp.zeros_like(ci_ref)

    u = u_ref[...]
    bu_ref[...] = _dot(u.astype(BF16), bblk_ref[...])
    pr = pr_ref[...]
    pi = pi_ref[...]

    def blk(i, carry):
        cr, ci = carry
        rows = pl.ds(pl.multiple_of(i * SUBLANES, SUBLANES), SUBLANES)
        xr = bu_ref[rows, 0:nstate]
        xi = bu_ref[rows, nstate:2 * nstate]
        for lv in range(3):
            sh = 1 << lv
            sr = pltpu.roll(xr, sh, 0)
            si = pltpu.roll(xi, sh, 0)
            tr = tr_ref[lv]
            ti = ti_ref[lv]
            xr, xi = xr + (tr * sr - ti * si), xi + (tr * si + ti * sr)
        xr, xi = xr + (pr * cr - pi * ci), xi + (pr * ci + pi * cr)
        bu_ref[rows, 0:nstate] = xr
        bu_ref[rows, nstate:2 * nstate] = xi
        return xr[SUBLANES - 1:SUBLANES, :], xi[SUBLANES - 1:SUBLANES, :]

    cr, ci = lax.fori_loop(0, nblk, blk, (cr_ref[...], ci_ref[...]))
    cr_ref[...] = cr
    ci_ref[...] = ci

    half = u.shape[0] // 2
    y = jnp.concatenate([_dot(bu_ref[r:r + half, :].astype(BF16), cblk_ref[...]) for r in (0, half)], axis=0)
    y = y + d_ref[...] * u
    y = jax.nn.gelu(y)
    z = y * jax.nn.sigmoid(_dot(y.astype(BF16), wglu_ref[...]))
    var = jnp.mean(z * z, axis=-1, keepdims=True)
    o_ref[...] = (z * lax.rsqrt(var + EPS) * gain_ref[...]).astype(BF16)


def _s5(u, a_re, a_im, log_dt, b_re, b_im, c_re, c_im, d_skip, w_glu, out_gain, *, tt):
    L, width = u.shape
    G, P = a_re.shape
    nstate = G * P
    lam_re = jnp.minimum(a_re.astype(F32), -S5_MIN_NEG)
    lam_im = a_im.astype(F32)
    dt = jnp.exp(log_dt.astype(F32))
    mag = jnp.exp(lam_re * dt)
    abar_re, abar_im = mag * jnp.cos(lam_im * dt), mag * jnp.sin(lam_im * dt)
    den = lam_re * lam_re + lam_im * lam_im
    sc_re = ((abar_re - 1.0) * lam_re + abar_im * lam_im) / den
    sc_im = (abar_im * lam_re - (abar_re - 1.0) * lam_im) / den
    bre, bim = b_re.astype(F32), b_im.astype(F32)
    bbar_re = bre * sc_re[..., None] - bim * sc_im[..., None]
    bbar_im = bre * sc_im[..., None] + bim * sc_re[..., None]
    eye = jnp.eye(G, dtype=F32)
    bd = lambda m: jnp.einsum('gph,gk->ghkp', m, eye).reshape(G * S5_GROUP, nstate)
    bblk = jnp.concatenate([bd(bbar_re), bd(bbar_im)], axis=1).astype(BF16)
    cd = lambda m: jnp.einsum('ghp,gk->gpkh', m, eye).reshape(nstate, G * S5_GROUP)
    cblk = jnp.concatenate([cd(c_re.astype(F32)), cd(-c_im.astype(F32))], axis=0).astype(BF16)
    kpow = jnp.arange(1, SUBLANES + 1, dtype=F32)[:, None]
    pmag = jnp.exp((lam_re * dt).reshape(1, nstate) * kpow)
    parg = (lam_im * dt).reshape(1, nstate) * kpow
    p_re, p_im = pmag * jnp.cos(parg), pmag * jnp.sin(parg)
    rowi = np.arange(SUBLANES)[:, None]
    tmask = jnp.asarray(np.stack([(rowi >= (1 << lv)).astype(np.float32) for lv in range(3)]))
    t_re = tmask * jnp.stack([p_re[(1 << lv) - 1] for lv in range(3)])[:, None, :]
    t_im = tmask * jnp.stack([p_im[(1 << lv) - 1] for lv in range(3)])[:, None, :]
    d2 = d_skip.astype(F32).reshape(1, width)
    gain2 = out_gain.astype(F32)[None, :]
    wglu = w_glu.astype(BF16)
    full = lambda a: pl.BlockSpec(a.shape, lambda i: (0,) * a.ndim)
    return pl.pallas_call(
        functools.partial(_s5_kernel, nblk=tt // SUBLANES, nstate=nstate),
        out_shape=jax.ShapeDtypeStruct((L, width), BF16),
        grid=(L // tt,),
        in_specs=[pl.BlockSpec((tt, width), lambda i: (i, 0)), full(bblk), full(cblk), full(d2),
                  full(t_re), full(t_im), full(p_re), full(p_im), full(wglu), full(gain2)],
        out_specs=pl.BlockSpec((tt, width), lambda i: (i, 0)),
        scratch_shapes=[pltpu.VMEM((tt, 2 * nstate), F32), pltpu.VMEM((1, nstate), F32), pltpu.VMEM((1, nstate), F32)],
        compiler_params=_cparams(("arbitrary",)),
        name="s5",
    )(u, bblk, cblk, d2, t_re, t_im, p_re, p_im, wglu, gain2)


ATTN_ROWS = 32


def _attn_kernel(posq_ref, posk_ref, q_ref, k_ref, vt_ref, lq1_ref, lk1_ref, lq2_ref, lk2_ref, sg_ref,
                 o_ref, m_ref, l_ref, acc_ref, sa_ref, sb_ref, mba_ref, mbb_ref, p_ref, q2t_ref, *, tq, lambda_init):
    qi = pl.program_id(1)
    qt = q_ref[0]
    feat = lax.broadcasted_iota(jnp.int32, qt.shape, 0)
    zero = jnp.zeros_like(qt)
    q2t_ref[...] = jnp.concatenate([jnp.where(feat < HEAD_DIM, qt, zero), jnp.where(feat < HEAD_DIM, zero, qt)], axis=1)

    def produce(j, s_dst, mb_dst):
        kb = k_ref[pl.ds(pl.multiple_of(j * tq, tq), tq), :]
        sc = _dot(kb, q2t_ref[...])
        s_dst[...] = sc
        mb_dst[...] = jnp.max(sc, axis=0, keepdims=True)

    def step(j, src, dst, mask):
        s_src, mb_src = src
        pv = _dot(vt_ref[jnp.maximum(j - 1, 0)], p_ref[...])
        if dst is not None:
            produce(j + 1, *dst)

        def rows(r):
            sc = s_src[r:r + ATTN_ROWS, :]
            if mask is None:
                return sc
            mk = mask[r:r + ATTN_ROWS, :]
            return jnp.where(jnp.concatenate([mk, mk], axis=1), sc, -jnp.inf)

        if mask is None:
            m_blk = mb_src[...]
        else:
            m_blk = jnp.full(m_ref.shape, -jnp.inf, F32)
            for r in range(0, tq, ATTN_ROWS):
                m_blk = jnp.maximum(m_blk, jnp.max(rows(r), axis=0, keepdims=True))
        m_old = m_ref[...]
        m_new = jnp.maximum(m_old, m_blk)
        alpha = jnp.exp2(m_old - m_new)
        lsum = jnp.zeros((SUBLANES, 2 * tq), F32)
        for r in range(0, tq, ATTN_ROWS):
            pc = jnp.exp2(rows(r) - m_new)
            lsum = lsum + jnp.sum(pc.reshape(ATTN_ROWS // SUBLANES, SUBLANES, 2 * tq), axis=0)
            p_ref[r:r + ATTN_ROWS, :] = pc.astype(BF16)
        l_ref[...] = alpha * l_ref[...] + jnp.sum(lsum, axis=0, keepdims=True)
        m_ref[...] = m_new
        acc_ref[...] = alpha * (acc_ref[...] + pv)

    buf_a = (sa_ref, mba_ref)
    buf_b = (sb_ref, mbb_ref)
    m_ref[...] = jnp.full(m_ref.shape, -jnp.inf, F32)
    l_ref[...] = jnp.zeros_like(l_ref)
    acc_ref[...] = jnp.zeros_like(acc_ref)
    p_ref[...] = jnp.zeros_like(p_ref)
    produce(0, *buf_a)

    def body(t, carry):
        step(2 * t, buf_a, buf_b, None)
        step(2 * t + 1, buf_b, buf_a, None)
        return carry

    lax.fori_loop(0, qi // 2, body, 0)

    mask = posk_ref[...] <= posq_ref[...]

    @pl.when(qi % 2 == 0)
    def _():
        step(qi, buf_a, None, mask)

    @pl.when(qi % 2 == 1)
    def _():
        step(qi - 1, buf_a, buf_b, None)
        step(qi, buf_b, None, mask)

    acc_ref[...] = acc_ref[...] + _dot(vt_ref[qi], p_ref[...])

    lam = (jnp.exp(jnp.sum(lq1_ref[...] * lk1_ref[...], axis=-1, keepdims=True))
           - jnp.exp(jnp.sum(lq2_ref[...] * lk2_ref[...], axis=-1, keepdims=True)) + lambda_init)
    ot = acc_ref[...] / l_ref[...]
    ot = ot[:, 0:tq] - lam * ot[:, tq:2 * tq]
    ms = jnp.mean(ot * ot, axis=0, keepdims=True)
    ot = ot * lax.rsqrt(ms + EPS) * (1.0 - lambda_init)
    o_ref[...] = (jnp.transpose(ot) * sg_ref[...]).astype(BF16)


def _attn(q, k, vt, pos_col, pos_row, lq1, lk1, lq2, lk2, subln_gain, *, lambda_init, tq):
    L, qkw = k.shape
    heads = qkw // (2 * HEAD_DIM)
    assert q.shape == (L // tq, qkw, tq)
    nblk = vt.shape[0]
    assert vt.shape[2] == tq and nblk * tq == L
    vec = lambda a: a.astype(F32)[None, :]
    lq1, lk1, lq2, lk2, sg = vec(lq1), vec(lk1), vec(lq2), vec(lk2), vec(subln_gain)
    full = lambda a: pl.BlockSpec(a.shape, lambda h, i: (0,) * a.ndim)
    return pl.pallas_call(
        functools.partial(_attn_kernel, tq=tq, lambda_init=lambda_init),
        out_shape=jax.ShapeDtypeStruct((L, qkw), BF16),
        grid=(heads, L // tq),
        in_specs=[pl.BlockSpec((1, tq), lambda h, i: (0, i)),
                  pl.BlockSpec((tq, 1), lambda h, i: (i, 0)),
                  pl.BlockSpec((1, LANES, tq), lambda h, i: (i, h, 0)),
                  pl.BlockSpec((L, LANES), lambda h, i: (0, h)),
                  pl.BlockSpec((nblk, LANES, tq), lambda h, i: (0, h, 0)),
                  full(lq1), full(lk1), full(lq2), full(lk2), full(sg)],
        out_specs=pl.BlockSpec((tq, LANES), lambda h, i: (i, h)),
        scratch_shapes=[pltpu.VMEM((1, 2 * tq), F32), pltpu.VMEM((1, 2 * tq), F32), pltpu.VMEM((LANES, 2 * tq), F32),
                        pltpu.VMEM((tq, 2 * tq), F32), pltpu.VMEM((tq, 2 * tq), F32),
                        pltpu.VMEM((1, 2 * tq), F32), pltpu.VMEM((1, 2 * tq), F32), pltpu.VMEM((tq, 2 * tq), BF16),
                        pltpu.VMEM((LANES, 2 * tq), BF16)],
        compiler_params=_cparams(("arbitrary", "arbitrary")),
        name="diff_attn",
    )(pos_row, pos_col, q, k, vt, lq1, lk1, lq2, lk2, sg)


def _mlp_kernel(x_ref, oa_ref, ob_ref, oc_ref, wout_ref, g_ref, wup_ref, wdn_ref, o_ref,
                xres_ref, h_ref, acc_ref, *, wa, wb):
    k = pl.program_id(1)

    @pl.when(k == 0)
    def _():
        xn = (x_ref[...] + _dot(oa_ref[...], wout_ref[0:wa, :]) + _dot(ob_ref[...], wout_ref[wa:wa + wb, :])
              + _dot(oc_ref[...], wout_ref[wa + wb:, :]))
        xres_ref[...] = xn
        var = jnp.mean(xn * xn, axis=-1, keepdims=True)
        h_ref[...] = (xn * lax.rsqrt(var + EPS) * g_ref[...]).astype(BF16)
        acc_ref[...] = jnp.zeros_like(acc_ref)

    up = _dot(h_ref[...], wup_ref[...])
    act = jnp.square(jnp.maximum(up, 0.0))
    acc_ref[...] += _dot(act.astype(BF16), wdn_ref[...])

    @pl.when(k == pl.num_programs(1) - 1)
    def _():
        o_ref[...] = xres_ref[...] + acc_ref[...]


def _mlp(x2, oa, ob, oc, wout, gain, wup, wdn, *, tm, tf):
    L, D = x2.shape
    dff = wup.shape[1]
    wa, wb, wc = oa.shape[1], ob.shape[1], oc.shape[1]
    gain2 = gain.astype(F32)[None, :]
    row = lambda w: pl.BlockSpec((tm, w), lambda i, k: (i, 0))
    return pl.pallas_call(
        functools.partial(_mlp_kernel, wa=wa, wb=wb),
        out_shape=jax.ShapeDtypeStruct((L, D), F32),
        grid=(L // tm, dff // tf),
        in_specs=[row(D), row(wa), row(wb), row(wc),
                  pl.BlockSpec((D, D), lambda i, k: (0, 0)), pl.BlockSpec((1, D), lambda i, k: (0, 0)),
                  pl.BlockSpec((D, tf), lambda i, k: (0, k)), pl.BlockSpec((tf, D), lambda i, k: (k, 0))],
        out_specs=row(D),
        scratch_shapes=[pltpu.VMEM((tm, D), F32), pltpu.VMEM((tm, D), BF16), pltpu.VMEM((tm, D), F32)],
        compiler_params=_cparams(("parallel", "arbitrary")),
        name="out_proj_mlp",
    )(x2, oa, ob, oc, wout, gain2, wup, wdn)


def kernel(x, positions, hgrn_lower_bounds, attn_norm_gain, w_in, hg_norm_gain, s5_a_re, s5_a_im, s5_log_dt, s5_b_re, s5_b_im, s5_c_re, s5_c_im, s5_d, s5_w_glu, s5_norm_gain, da_q_norm_gain, da_k_norm_gain, da_lambda_q1, da_lambda_k1, da_lambda_q2, da_lambda_k2, da_subln_gain, w_out, mlp_norm_gain, w_mlp_up, w_mlp_down):
    batch, L, D = x.shape
    assert batch == 1
    depth = w_in.shape[0]
    tile = min(L, 512)
    x2 = x.reshape(L, D)
    pos_col = positions.reshape(L, 1)
    pos_row = positions.reshape(1, L)
    cos, sin = _rope_tables(pos_col)
    for l in range(depth):
        lambda_init = 0.8 - 0.6 * math.exp(-0.3 * l)
        hg, su, q, k, v = _proj(x2, attn_norm_gain[l], w_in[l].astype(BF16), cos, sin,
                                da_q_norm_gain[l], da_k_norm_gain[l], tm=tile)
        o_a = _hgrn(hg, hgrn_lower_bounds, hg_norm_gain[l], layer=l, tt=tile)
        o_b = _s5(su, s5_a_re[l], s5_a_im[l], s5_log_dt[l], s5_b_re[l], s5_b_im[l], s5_c_re[l], s5_c_im[l],
                  s5_d[l], s5_w_glu[l], s5_norm_gain[l], tt=tile)
        o_c = _attn(q, k, v, pos_col, pos_row, da_lambda_q1[l], da_lambda_k1[l], da_lambda_q2[l],
                    da_lambda_k2[l], da_subln_gain[l], lambda_init=lambda_init, tq=tile)
        x2 = _mlp(x2, o_a, o_b, o_c, w_out[l].astype(BF16), mlp_norm_gain[l],
                  w_mlp_up[l].astype(BF16), w_mlp_down[l].astype(BF16), tm=tile, tf=2048)
    return x2.reshape(batch, L, D)
```

```python
import functools
import math

import numpy as np
import jax
import jax.numpy as jnp
from jax import lax
from jax.experimental import pallas as pl
from jax.experimental.pallas import tpu as pltpu

F32 = jnp.float32
BF16 = jnp.bfloat16

EPS = 1e-6
ROPE_THETA = 10000.0
S5_MIN_NEG = 1e-4
LOG2_E = math.log2(math.e)

HEAD_DIM = 64
HG_CHUNK = 128
HG_LEVELS = 7
S5_GROUP = 16
S5_STATE = 64
SUBLANES = 8
LANES = 128
MXU_TILE = 256

VMEM_LIMIT = 56 * 1024 * 1024


def _cparams(semantics):
    return pltpu.CompilerParams(dimension_semantics=semantics, vmem_limit_bytes=VMEM_LIMIT)


def _dot(a, b):
    return jnp.dot(a, b, preferred_element_type=F32)


def _dot_nt(a, b):
    return lax.dot_general(a, b, (((1,), (1,)), ((), ())), preferred_element_type=F32)


def _dot_tn(a, b):
    return lax.dot_general(a, b, (((0,), (0,)), ((), ())), preferred_element_type=F32)


def _split_bf16(t):
    hi = t.astype(BF16)
    lo = (t - hi.astype(F32)).astype(BF16)
    return hi, lo


def _group_sum(t, ones_blockdiag):
    hi, lo = _split_bf16(t)
    return _dot(hi, ones_blockdiag) + _dot(lo, ones_blockdiag)


def _rope_kernel(pos_ref, invf_ref, cos_ref, sin_ref):
    ang = pos_ref[...].astype(F32) * invf_ref[...]
    lane = lax.broadcasted_iota(jnp.int32, ang.shape, 1)
    first_half = (lane % HEAD_DIM) < (HEAD_DIM // 2)
    s = jnp.sin(ang)
    cos_ref[...] = jnp.cos(ang)
    sin_ref[...] = jnp.where(first_half, -s, s)


def _rope_tables(pos_col):
    L = pos_col.shape[0]
    tr = min(L, 2048)
    inv_freq = ROPE_THETA ** (-jnp.arange(0, HEAD_DIM, 2, dtype=F32) / HEAD_DIM)
    invf = jnp.tile(inv_freq, LANES // (HEAD_DIM // 2))[None, :]
    return pl.pallas_call(
        _rope_kernel,
        out_shape=(jax.ShapeDtypeStruct((L, LANES), F32), jax.ShapeDtypeStruct((L, LANES), F32)),
        grid=(L // tr,),
        in_specs=[pl.BlockSpec((tr, 1), lambda i: (i, 0)), pl.BlockSpec((1, LANES), lambda i: (0, 0))],
        out_specs=(pl.BlockSpec((tr, LANES), lambda i: (i, 0)), pl.BlockSpec((tr, LANES), lambda i: (i, 0))),
        compiler_params=_cparams(("parallel",)),
        name="rope_tables",
    )(pos_col, invf)


def _proj_kernel(x_ref, g_ref, w_ref, cos_ref, sin_ref, qg_ref, kg_ref, ones_ref,
                 hg_ref, su_ref, q_ref, k_ref, v_ref, *, splits):
    x = x_ref[...]
    var = jnp.mean(x * x, axis=-1, keepdims=True)
    h = (x * lax.rsqrt(var + EPS) * g_ref[...]).astype(BF16)
    c0, c1, c2, c3, c4, c5 = splits
    hg_ref[...] = _dot(h, w_ref[:, c0:c1])
    su_ref[...] = _dot(h, w_ref[:, c1:c2])
    v_ref[...] = _dot(h, w_ref[:, c4:c5]).astype(BF16)

    cos = cos_ref[...]
    sin = sin_ref[...]
    lane = lax.broadcasted_iota(jnp.int32, cos.shape, 1)
    first_half = (lane % HEAD_DIM) < (HEAD_DIM // 2)
    ones = ones_ref[...]

    def norm_rope(t, gain, scale):
        tt = t * t
        ms = jnp.concatenate([_group_sum(tt[:, c:c + MXU_TILE], ones) for c in range(0, t.shape[1], MXU_TILE)], axis=1)
        tn = t * lax.rsqrt(ms * (1.0 / HEAD_DIM) + EPS) * gain
        outs = []
        for j in range(t.shape[1] // LANES):
            tj = tn[:, j * LANES:(j + 1) * LANES]
            partner = jnp.where(first_half,
                                pltpu.roll(tj, LANES - HEAD_DIM // 2, 1),
                                pltpu.roll(tj, HEAD_DIM // 2, 1))
            outs.append((tj * cos + partner * sin) * scale)
        return jnp.concatenate(outs, axis=1)

    q = norm_rope(_dot(h, w_ref[:, c2:c3]), qg_ref[...], LOG2_E * HEAD_DIM ** -0.5)
    q_ref[0] = jnp.transpose(q).astype(BF16)
    k_ref[...] = norm_rope(_dot(h, w_ref[:, c3:c4]), kg_ref[...], 1.0).astype(BF16)


def _proj(x2, gain, w_bf16, cos, sin, q_gain, k_gain, *, tm):
    L, D = x2.shape
    ncols = w_bf16.shape[1]
    hgw, s5w, qkw, vw = 4 * (D // 4), D // 4, D // 2, D // 2
    splits = (0, hgw, hgw + s5w, hgw + s5w + qkw, hgw + s5w + 2 * qkw, hgw + s5w + 2 * qkw + vw)
    assert splits[-1] == ncols
    assert qkw % MXU_TILE == 0
    ones = jnp.asarray(np.kron(np.eye(MXU_TILE // HEAD_DIM), np.ones((HEAD_DIM, HEAD_DIM))).astype(np.float32)).astype(BF16)
    qg = jnp.tile(q_gain.astype(F32), qkw // HEAD_DIM)[None, :]
    kg = jnp.tile(k_gain.astype(F32), qkw // HEAD_DIM)[None, :]
    row = lambda w: pl.BlockSpec((tm, w), lambda i: (i, 0))
    full = lambda a: pl.BlockSpec(a.shape, lambda i: (0,) * a.ndim)
    gain2 = gain.astype(F32)[None, :]
    return pl.pallas_call(
        functools.partial(_proj_kernel, splits=splits),
        out_shape=(jax.ShapeDtypeStruct((L, hgw), F32), jax.ShapeDtypeStruct((L, s5w), F32),
                   jax.ShapeDtypeStruct((L // tm, qkw, tm), BF16), jax.ShapeDtypeStruct((L, qkw), BF16),
                   jax.ShapeDtypeStruct((L, vw), BF16)),
        grid=(L // tm,),
        in_specs=[row(D), full(gain2), full(w_bf16), row(LANES), row(LANES), full(qg), full(kg), full(ones)],
        out_specs=(row(hgw), row(s5w), pl.BlockSpec((1, qkw, tm), lambda i: (i, 0, 0)), row(qkw),
                   row(vw)),
        compiler_params=_cparams(("parallel",)),
        name="in_proj",
    )(x2, gain2, w_bf16, cos, sin, qg, kg, ones)


def _hgrn_kernel(hg_ref, lbraw_ref, gn_ref, mstack_ref, pmask_ref, ones_ref, o_ref, st_ref,
                 *, layer, nchunk, width, heads):
    @pl.when(pl.program_id(0) == 0)
    def _():
        st_ref[...] = jnp.zeros_like(st_ref)

    lbr = lbraw_ref[...]
    e = jnp.exp(lbr - jnp.max(lbr, axis=0, keepdims=True))
    sm = e / jnp.sum(e, axis=0, keepdims=True)
    lb = jnp.zeros((1, width), F32)
    for r in range(1, layer + 1):
        lb = lb + sm[r:r + 1, :]
    lbf = jnp.clip(lb, 0.0, 1.0)
    log_lb = jnp.log(lbf)
    log_1m_lb = jnp.log1p(-lbf)

    row = lax.broadcasted_iota(jnp.int32, (HG_CHUNK, 1), 0)
    gn = gn_ref[...]
    ones = ones_ref[...]

    def chunk(c, carry):
        rows = pl.ds(pl.multiple_of(c * HG_CHUNK, HG_CHUNK), HG_CHUNK)
        q = hg_ref[rows, 0:width]
        fr = hg_ref[rows, width:2 * width]
        vi = hg_ref[rows, 2 * width:3 * width]
        g = hg_ref[rows, 3 * width:4 * width]
        qh = q * jax.nn.sigmoid(q)
        log_sig = jnp.minimum(fr, 0.0) - jnp.log1p(jnp.exp(-jnp.abs(fr)))
        t2 = log_1m_lb + log_sig
        lf = jnp.maximum(log_lb, t2) + jnp.log1p(jnp.exp(-jnp.abs(log_lb - t2)))
        kh = (1.0 - lbf) * jax.nn.sigmoid(-fr)

        hi, lo = _split_bf16(lf)
        r = _dot(mstack_ref[...], jnp.concatenate([hi, lo], axis=1))
        r = r[:, 0:width] + r[:, width:2 * width]
        b = r[0:HG_CHUNK]
        b_last = b[HG_CHUNK - 1:HG_CHUNK, :]

        qls, kls = [], []
        for lv in range(HG_LEVELS):
            hblk = 1 << lv
            w = jnp.exp(-jnp.abs(b - r[(lv + 1) * HG_CHUNK:(lv + 2) * HG_CHUNK]))
            odd = ((row // hblk) % 2) == 1
            qls.append(jnp.where(odd, qh * w, 0.0).astype(BF16))
            kls.append(jnp.where(odd, 0.0, kh * w).astype(BF16))

        o_diag = _group_sum(qh * kh, ones) * vi
        qe = (qh * jnp.exp(b)).astype(BF16)
        kd = (kh * jnp.exp(b_last - b)).astype(BF16)
        dec = jnp.exp(b_last)
        vb = vi.astype(BF16)

        outs = []
        for hd in range(heads):
            hs = slice(hd * HEAD_DIM, (hd + 1) * HEAD_DIM)
            sc = jnp.zeros((HG_CHUNK, HG_CHUNK), F32)
            for lv in range(HG_LEVELS):
                sc = sc + pmask_ref[lv] * _dot_nt(qls[lv][:, hs], kls[lv][:, hs])
            st = st_ref[hd]
            v_h = vb[:, hs]
            o_h = _dot(sc.astype(BF16), v_h) + _dot_nt(qe[:, hs], st.astype(BF16)) + o_diag[:, hs]
            st_ref[hd] = st * dec[:, hs] + _dot_tn(v_h, kd[:, hs])
            ms = jnp.mean(o_h * o_h, axis=-1, keepdims=True)
            outs.append(o_h * lax.rsqrt(ms + EPS) * gn)
        o = jnp.concatenate(outs, axis=1) * (g * jax.nn.sigmoid(g))
        o_ref[rows, :] = o.astype(BF16)
        return carry

    lax.fori_loop(0, nchunk, chunk, 0, unroll=2)


def _hgrn_constants():
    t = np.arange(HG_CHUNK)
    mats = [(t[None, :] <= t[:, None]).astype(np.float32)]
    pmasks = []
    for lv in range(HG_LEVELS):
        hblk = 1 << lv
        boundary = (t // (2 * hblk)) * 2 * hblk + hblk - 1
        mats.append((t[None, :] <= boundary[:, None]).astype(np.float32))
        pmasks.append((t[None, :] // (2 * hblk) == t[:, None] // (2 * hblk)).astype(np.float32))
    return np.concatenate(mats, axis=0), np.stack(pmasks, axis=0)


def _hgrn(hg, lb_raw, gn_gain, *, layer, tt):
    L, w4 = hg.shape
    width = w4 // 4
    heads = width // HEAD_DIM
    mstack_np, pmask_np = _hgrn_constants()
    mstack = jnp.asarray(mstack_np).astype(BF16)
    pmask = jnp.asarray(pmask_np)
    ones = jnp.asarray(np.kron(np.eye(heads), np.ones((HEAD_DIM, HEAD_DIM))).astype(np.float32)).astype(BF16)
    lbr = lb_raw.astype(F32)
    gn = gn_gain.astype(F32)[None, :]
    full = lambda a: pl.BlockSpec(a.shape, lambda i: (0,) * a.ndim)
    return pl.pallas_call(
        functools.partial(_hgrn_kernel, layer=layer, nchunk=tt // HG_CHUNK, width=width, heads=heads),
        out_shape=jax.ShapeDtypeStruct((L, width), BF16),
        grid=(L // tt,),
        in_specs=[pl.BlockSpec((tt, w4), lambda i: (i, 0)), full(lbr), full(gn), full(mstack), full(pmask), full(ones)],
        out_specs=pl.BlockSpec((tt, width), lambda i: (i, 0)),
        scratch_shapes=[pltpu.VMEM((heads, HEAD_DIM, HEAD_DIM), F32)],
        compiler_params=_cparams(("arbitrary",)),
        name="hgrn2",
    )(hg, lbr, gn, mstack, pmask, ones)


def _s5_kernel(u_ref, bblk_ref, cblk_ref, d_ref, tr_ref, ti_ref, pr_ref, pi_ref, wglu_ref, gain_ref,
               o_ref, bu_ref, cr_ref, ci_ref, *, nblk, nstate):
    @pl.when(pl.program_id(0) == 0)
    def _():
        cr_ref[...] = jnp.zeros_like(cr_ref)
        ci_ref[...] = jnp.zeros_like(ci_ref)

    u = u_ref[...]
    bu_ref[...] = _dot(u.astype(BF16), bblk_ref[...])
    pr = pr_ref[...]
    pi = pi_ref[...]

    def blk(i, carry):
        cr, ci = carry
        rows = pl.ds(pl.multiple_of(i * SUBLANES, SUBLANES), SUBLANES)
        xr = bu_ref[rows, 0:nstate]
        xi = bu_ref[rows, nstate:2 * nstate]
        for lv in range(3):
            sh = 1 << lv
            sr = pltpu.roll(xr, sh, 0)
            si = pltpu.roll(xi, sh, 0)
            tr = tr_ref[lv]
            ti = ti_ref[lv]
            xr, xi = xr + (tr * sr - ti * si), xi + (tr * si + ti * sr)
        xr, xi = xr + (pr * cr - pi * ci), xi + (pr * ci + pi * cr)
        bu_ref[rows, 0:nstate] = xr
        bu_ref[rows, nstate:2 * nstate] = xi
        return xr[SUBLANES - 1:SUBLANES, :], xi[SUBLANES - 1:SUBLANES, :]

    cr, ci = lax.fori_loop(0, nblk, blk, (cr_ref[...], ci_ref[...]))
    cr_ref[...] = cr
    ci_ref[...] = ci

    half = u.shape[0] // 2
    y = jnp.concatenate([_dot(bu_ref[r:r + half, :].astype(BF16), cblk_ref[...]) for r in (0, half)], axis=0)
    y = y + d_ref[...] * u
    y = jax.nn.gelu(y)
    z = y * jax.nn.sigmoid(_dot(y.astype(BF16), wglu_ref[...]))
    var = jnp.mean(z * z, axis=-1, keepdims=True)
    o_ref[...] = (z * lax.rsqrt(var + EPS) * gain_ref[...]).astype(BF16)


def _s5(u, a_re, a_im, log_dt, b_re, b_im, c_re, c_im, d_skip, w_glu, out_gain, *, tt):
    L, width = u.shape
    G, P = a_re.shape
    nstate = G * P
    lam_re = jnp.minimum(a_re.astype(F32), -S5_MIN_NEG)
    lam_im = a_im.astype(F32)
    dt = jnp.exp(log_dt.astype(F32))
    mag = jnp.exp(lam_re * dt)
    abar_re, abar_im = mag * jnp.cos(lam_im * dt), mag * jnp.sin(lam_im * dt)
    den = lam_re * lam_re + lam_im * lam_im
    sc_re = ((abar_re - 1.0) * lam_re + abar_im * lam_im) / den
    sc_im = (abar_im * lam_re - (abar_re - 1.0) * lam_im) / den
    bre, bim = b_re.astype(F32), b_im.astype(F32)
    bbar_re = bre * sc_re[..., None] - bim * sc_im[..., None]
    bbar_im = bre * sc_im[..., None] + bim * sc_re[..., None]
    eye = jnp.eye(G, dtype=F32)
    bd = lambda m: jnp.einsum('gph,gk->ghkp', m, eye).reshape(G * S5_GROUP, nstate)
    bblk = jnp.concatenate([bd(bbar_re), bd(bbar_im)], axis=1).astype(BF16)
    cd = lambda m: jnp.einsum('ghp,gk->gpkh', m, eye).reshape(nstate, G * S5_GROUP)
    cblk = jnp.concatenate([cd(c_re.astype(F32)), cd(-c_im.astype(F32))], axis=0).astype(BF16)
    kpow = jnp.arange(1, SUBLANES + 1, dtype=F32)[:, None]
    pmag = jnp.exp((lam_re * dt).reshape(1, nstate) * kpow)
    parg = (lam_im * dt).reshape(1, nstate) * kpow
    p_re, p_im = pmag * jnp.cos(parg), pmag * jnp.sin(parg)
    rowi = np.arange(SUBLANES)[:, None]
    tmask = jnp.asarray(np.stack([(rowi >= (1 << lv)).astype(np.float32) for lv in range(3)]))
    t_re = tmask * jnp.stack([p_re[(1 << lv) - 1] for lv in range(3)])[:, None, :]
    t_im = tmask * jnp.stack([p_im[(1 << lv) - 1] for lv in range(3)])[:, None, :]
    d2 = d_skip.astype(F32).reshape(1, width)
    gain2 = out_gain.astype(F32)[None, :]
    wglu = w_glu.astype(BF16)
    full = lambda a: pl.BlockSpec(a.shape, lambda i: (0,) * a.ndim)
    return pl.pallas_call(
        functools.partial(_s5_kernel, nblk=tt // SUBLANES, nstate=nstate),
        out_shape=jax.ShapeDtypeStruct((L, width), BF16),
        grid=(L // tt,),
        in_specs=[pl.BlockSpec((tt, width), lambda i: (i, 0)), full(bblk), full(cblk), full(d2),
                  full(t_re), full(t_im), full(p_re), full(p_im), full(wglu), full(gain2)],
        out_specs=pl.BlockSpec((tt, width), lambda i: (i, 0)),
        scratch_shapes=[pltpu.VMEM((tt, 2 * nstate), F32), pltpu.VMEM((1, nstate), F32), pltpu.VMEM((1, nstate), F32)],
        compiler_params=_cparams(("arbitrary",)),
        name="s5",
    )(u, bblk, cblk, d2, t_re, t_im, p_re, p_im, wglu, gain2)


ATTN_ROWS = 32


def _attn_kernel(posq_ref, posk_ref, q_ref, k_ref, v_ref, lq1_ref, lk1_ref, lq2_ref, lk2_ref, sg_ref,
                 o_ref, m_ref, l_ref, acc_ref, sa_ref, sb_ref, mba_ref, mbb_ref, p_ref, q2t_ref, *, tq, lambda_init):
    qi = pl.program_id(1)
    qt = q_ref[0]
    feat = lax.broadcasted_iota(jnp.int32, qt.shape, 0)
    zero = jnp.zeros_like(qt)
    q2t_ref[...] = jnp.concatenate([jnp.where(feat < HEAD_DIM, qt, zero), jnp.where(feat < HEAD_DIM, zero, qt)], axis=1)

    def values(j):
        vb = v_ref[pl.ds(pl.multiple_of(j * tq, tq), tq), :]
        return jnp.transpose(_dot_tn(p_ref[...], vb))

    def produce(j, s_dst, mb_dst):
        kb = k_ref[pl.ds(pl.multiple_of(j * tq, tq), tq), :]
        sc = _dot(kb, q2t_ref[...])
        s_dst[...] = sc
        mb_dst[...] = jnp.max(sc, axis=0, keepdims=True)

    def step(j, src, dst, mask):
        s_src, mb_src = src
        pv = values(jnp.maximum(j - 1, 0))
        if dst is not None:
            produce(j + 1, *dst)

        def rows(r):
            sc = s_src[r:r + ATTN_ROWS, :]
            if mask is None:
                return sc
            mk = mask[r:r + ATTN_ROWS, :]
            return jnp.where(jnp.concatenate([mk, mk], axis=1), sc, -jnp.inf)

        if mask is None:
            m_blk = mb_src[...]
        else:
            m_blk = jnp.full(m_ref.shape, -jnp.inf, F32)
            for r in range(0, tq, ATTN_ROWS):
                m_blk = jnp.maximum(m_blk, jnp.max(rows(r), axis=0, keepdims=True))
        m_old = m_ref[...]
        m_new = jnp.maximum(m_old, m_blk)
        alpha = jnp.exp2(m_old - m_new)
        lsum = jnp.zeros((SUBLANES, 2 * tq), F32)
        for r in range(0, tq, ATTN_ROWS):
            pc = jnp.exp2(rows(r) - m_new)
            lsum = lsum + jnp.sum(pc.reshape(ATTN_ROWS // SUBLANES, SUBLANES, 2 * tq), axis=0)
            p_ref[r:r + ATTN_ROWS, :] = pc.astype(BF16)
        l_ref[...] = alpha * l_ref[...] + jnp.sum(lsum, axis=0, keepdims=True)
        m_ref[...] = m_new
        acc_ref[...] = alpha * (acc_ref[...] + pv)

    buf_a = (sa_ref, mba_ref)
    buf_b = (sb_ref, mbb_ref)
    m_ref[...] = jnp.full(m_ref.shape, -jnp.inf, F32)
    l_ref[...] = jnp.zeros_like(l_ref)
    acc_ref[...] = jnp.zeros_like(acc_ref)
    p_ref[...] = jnp.zeros_like(p_ref)
    produce(0, *buf_a)

    def body(t, carry):
        step(2 * t, buf_a, buf_b, None)
        step(2 * t + 1, buf_b, buf_a, None)
        return carry

    lax.fori_loop(0, qi // 2, body, 0)

    mask = posk_ref[...] <= posq_ref[...]

    @pl.when(qi % 2 == 0)
    def _():
        step(qi, buf_a, None, mask)

    @pl.when(qi % 2 == 1)
    def _():
        step(qi - 1, buf_a, buf_b, None)
        step(qi, buf_b, None, mask)

    acc_ref[...] = acc_ref[...] + values(qi)

    lam = (jnp.exp(jnp.sum(lq1_ref[...] * lk1_ref[...], axis=-1, keepdims=True))
           - jnp.exp(jnp.sum(lq2_ref[...] * lk2_ref[...], axis=-1, keepdims=True)) + lambda_init)
    ot = acc_ref[...] / l_ref[...]
    ot = ot[:, 0:tq] - lam * ot[:, tq:2 * tq]
    ms = jnp.mean(ot * ot, axis=0, keepdims=True)
    ot = ot * lax.rsqrt(ms + EPS) * (1.0 - lambda_init)
    o_ref[...] = (jnp.transpose(ot) * sg_ref[...]).astype(BF16)


def _attn(q, k, v, pos_col, pos_row, lq1, lk1, lq2, lk2, subln_gain, *, lambda_init, tq):
    L, qkw = k.shape
    heads = qkw // (2 * HEAD_DIM)
    assert q.shape == (L // tq, qkw, tq)
    assert v.shape == (L, qkw)
    vec = lambda a: a.astype(F32)[None, :]
    lq1, lk1, lq2, lk2, sg = vec(lq1), vec(lk1), vec(lq2), vec(lk2), vec(subln_gain)
    full = lambda a: pl.BlockSpec(a.shape, lambda h, i: (0,) * a.ndim)
    return pl.pallas_call(
        functools.partial(_attn_kernel, tq=tq, lambda_init=lambda_init),
        out_shape=jax.ShapeDtypeStruct((L, qkw), BF16),
        grid=(heads, L // tq),
        in_specs=[pl.BlockSpec((1, tq), lambda h, i: (0, i)),
                  pl.BlockSpec((tq, 1), lambda h, i: (i, 0)),
                  pl.BlockSpec((1, LANES, tq), lambda h, i: (i, h, 0)),
                  pl.BlockSpec((L, LANES), lambda h, i: (0, h)),
                  pl.BlockSpec((L, LANES), lambda h, i: (0, h)),
                  full(lq1), full(lk1), full(lq2), full(lk2), full(sg)],
        out_specs=pl.BlockSpec((tq, LANES), lambda h, i: (i, h)),
        scratch_shapes=[pltpu.VMEM((1, 2 * tq), F32), pltpu.VMEM((1, 2 * tq), F32), pltpu.VMEM((LANES, 2 * tq), F32),
                        pltpu.VMEM((tq, 2 * tq), F32), pltpu.VMEM((tq, 2 * tq), F32),
                        pltpu.VMEM((1, 2 * tq), F32), pltpu.VMEM((1, 2 * tq), F32), pltpu.VMEM((tq, 2 * tq), BF16),
                        pltpu.VMEM((LANES, 2 * tq), BF16)],
        compiler_params=_cparams(("arbitrary", "arbitrary")),
        name="diff_attn",
    )(pos_row, pos_col, q, k, v, lq1, lk1, lq2, lk2, sg)


def _mlp_kernel(x_ref, oa_ref, ob_ref, oc_ref, wout_ref, g_ref, wup_ref, wdn_ref, o_ref,
                xres_ref, h_ref, acc_ref, *, wa, wb):
    k = pl.program_id(1)

    @pl.when(k == 0)
    def _():
        xn = (x_ref[...] + _dot(oa_ref[...], wout_ref[0:wa, :]) + _dot(ob_ref[...], wout_ref[wa:wa + wb, :])
              + _dot(oc_ref[...], wout_ref[wa + wb:, :]))
        xres_ref[...] = xn
        var = jnp.mean(xn * xn, axis=-1, keepdims=True)
        h_ref[...] = (xn * lax.rsqrt(var + EPS) * g_ref[...]).astype(BF16)
        acc_ref[...] = jnp.zeros_like(acc_ref)

    up = _dot(h_ref[...], wup_ref[...])
    act = jnp.square(jnp.maximum(up, 0.0))
    acc_ref[...] += _dot(act.astype(BF16), wdn_ref[...])

    @pl.when(k == pl.num_programs(1) - 1)
    def _():
        o_ref[...] = xres_ref[...] + acc_ref[...]


def _mlp(x2, oa, ob, oc, wout, gain, wup, wdn, *, tm, tf):
    L, D = x2.shape
    dff = wup.shape[1]
    wa, wb, wc = oa.shape[1], ob.shape[1], oc.shape[1]
    gain2 = gain.astype(F32)[None, :]
    row = lambda w: pl.BlockSpec((tm, w), lambda i, k: (i, 0))
    return pl.pallas_call(
        functools.partial(_mlp_kernel, wa=wa, wb=wb),
        out_shape=jax.ShapeDtypeStruct((L, D), F32),
        grid=(L // tm, dff // tf),
        in_specs=[row(D), row(wa), row(wb), row(wc),
                  pl.BlockSpec((D, D), lambda i, k: (0, 0)), pl.BlockSpec((1, D), lambda i, k: (0, 0)),
                  pl.BlockSpec((D, tf), lambda i, k: (0, k)), pl.BlockSpec((tf, D), lambda i, k: (k, 0))],
        out_specs=row(D),
        scratch_shapes=[pltpu.VMEM((tm, D), F32), pltpu.VMEM((tm, D), BF16), pltpu.VMEM((tm, D), F32)],
        compiler_params=_cparams(("parallel", "arbitrary")),
        name="out_proj_mlp",
    )(x2, oa, ob, oc, wout, gain2, wup, wdn)


def kernel(x, positions, hgrn_lower_bounds, attn_norm_gain, w_in, hg_norm_gain, s5_a_re, s5_a_im, s5_log_dt, s5_b_re, s5_b_im, s5_c_re, s5_c_im, s5_d, s5_w_glu, s5_norm_gain, da_q_norm_gain, da_k_norm_gain, da_lambda_q1, da_lambda_k1, da_lambda_q2, da_lambda_k2, da_subln_gain, w_out, mlp_norm_gain, w_mlp_up, w_mlp_down):
    batch, L, D = x.shape
    assert batch == 1
    depth = w_in.shape[0]
    tile = min(L, 512)
    x2 = x.reshape(L, D)
    pos_col = positions.reshape(L, 1)
    pos_row = positions.reshape(1, L)
    cos, sin = _rope_tables(pos_col)
    for l in range(depth):
        lambda_init = 0.8 - 0.6 * math.exp(-0.3 * l)
        hg, su, q, k, v = _proj(x2, attn_norm_gain[l], w_in[l].astype(BF16), cos, sin,
                                da_q_norm_gain[l], da_k_norm_gain[l], tm=tile)
        o_a = _hgrn(hg, hgrn_lower_bounds, hg_norm_gain[l], layer=l, tt=tile)
        o_b = _s5(su, s5_a_re[l], s5_a_im[l], s5_log_dt[l], s5_b_re[l], s5_b_im[l], s5_c_re[l], s5_c_im[l],
                  s5_d[l], s5_w_glu[l], s5_norm_gain[l], tt=tile)
        o_c = _attn(q, k, v, pos_col, pos_row, da_lambda_q1[l], da_lambda_k1[l], da_lambda_q2[l],
                    da_lambda_k2[l], da_subln_gain[l], lambda_init=lambda_init, tq=tile)
        x2 = _mlp(x2, o_a, o_b, o_c, w_out[l].astype(BF16), mlp_norm_gain[l],
                  w_mlp_up[l].astype(BF16), w_mlp_down[l].astype(BF16), tm=tile, tf=2048)
    return x2.reshape(batch, L, D)
```
